```python
import jax
import jax.numpy as jnp
from jax import lax
import numpy as np

D_MODEL = 4096
BATCH = 2
SEQ = 8192
DEPTH = 2

GRID_W = 64
CTX_LEN = 256
ROPE_THETA = 10000.0
Q_BLOCK = 128
EPS = 1e-6

GQA_HEADS = 16
GQA_KV_HEADS = 4
GQA_HEAD_DIM = 128
MLA_HEADS = 16
MLA_Q_LORA = 1024
MLA_KV_LORA = 512
MLA_NOPE = 128
MLA_ROPE = 64
MLA_V = 128
ATT_MIX = GQA_HEADS * GQA_HEAD_DIM + MLA_HEADS * MLA_V
ATT_SIZES = [GQA_HEADS * GQA_HEAD_DIM, GQA_KV_HEADS * GQA_HEAD_DIM, GQA_KV_HEADS * GQA_HEAD_DIM,
             MLA_Q_LORA, MLA_KV_LORA, MLA_ROPE]
ATT_IN = sum(ATT_SIZES) + ATT_MIX

GLA_HEADS = 6
GLA_DK = 256
GLA_DV = 512
GLA_GATE_RANK = 16
GLA_GATE_TAU = 16.0
GLA_CHUNK = 64
FNET_GROUPS = 4
FNET_DIM = 256
REC_MIX = GLA_HEADS * GLA_DV + FNET_GROUPS * FNET_DIM
REC_SIZES = [GLA_HEADS * GLA_DK, GLA_HEADS * GLA_DK, GLA_HEADS * GLA_DV,
             GLA_GATE_RANK, GLA_GATE_RANK, FNET_GROUPS * FNET_DIM]
REC_IN = sum(REC_SIZES) + REC_MIX

N_EVEN = (DEPTH + 1) // 2
N_ODD = DEPTH // 2

kernel_name = 'hybrid_gqa_mla_gla_fnet_prefix_dit'

F32 = jnp.float32


def rms_norm(x, g):
    xf = x.astype(F32)
    y = xf * lax.rsqrt(jnp.mean(xf * xf, axis=-1, keepdims=True) + EPS)
    return (y * g.astype(F32)).astype(x.dtype)


def split_cols(p, sizes):
    return jnp.split(p, [int(s) for s in np.cumsum(sizes)], axis=-1)


def axial_rope(n, rot_dim):
    rows = n // GRID_W
    row = jnp.repeat(jnp.arange(rows, dtype=F32), GRID_W)
    col = jnp.tile(jnp.arange(GRID_W, dtype=F32), rows)
    quarter = rot_dim // 4
    inv_freq = ROPE_THETA ** (-jnp.arange(quarter, dtype=F32) / quarter)
    ang = jnp.concatenate([row[:, None] * inv_freq, col[:, None] * inv_freq], axis=-1)
    return jnp.cos(ang), jnp.sin(ang)


def apply_rope(x, rope):
    if rope is None:
        return x
    cos, sin = rope
    cos = cos[None, :, None, :]
    sin = sin[None, :, None, :]
    x1, x2 = jnp.split(x.astype(F32), 2, axis=-1)
    return jnp.concatenate([x1 * cos - x2 * sin, x1 * sin + x2 * cos], axis=-1).astype(x.dtype)


def block_attention(q, k, v, scale):
    b, sq, hkv, grp, dh = q.shape
    nb = sq // Q_BLOCK
    qb = q.reshape(b, nb, Q_BLOCK, hkv, grp, dh).transpose(1, 0, 2, 3, 4, 5)

    def one_block(qblk):
        s = jnp.einsum('bqhgd,bkhd->bhgqk', qblk, k, preferred_element_type=F32) * scale
        p = jax.nn.softmax(s, axis=-1).astype(v.dtype)
        return jnp.einsum('bhgqk,bkhd->bqhgd', p, v)

    o = lax.map(one_block, qb)
    return o.transpose(1, 0, 2, 3, 4, 5).reshape(b, sq, hkv * grp * v.shape[-1])


def attn_prep(h, w_in, qn_g, kn_g, cq_g, ckv_g, w_uq, w_ukv, rope_a, rope_b):
    b, s, _ = h.shape
    qa, ka, va, cq, ckv, kr, gate = split_cols(h @ w_in, ATT_SIZES)
    qa = apply_rope(rms_norm(qa.reshape(b, s, GQA_HEADS, GQA_HEAD_DIM), qn_g), rope_a)
    ka = apply_rope(rms_norm(ka.reshape(b, s, GQA_KV_HEADS, GQA_HEAD_DIM), kn_g), rope_a)
    va = va.reshape(b, s, GQA_KV_HEADS, GQA_HEAD_DIM)
    qb = (rms_norm(cq, cq_g) @ w_uq).reshape(b, s, MLA_HEADS, MLA_NOPE + MLA_ROPE)
    qb = jnp.concatenate([qb[..., :MLA_NOPE], apply_rope(qb[..., MLA_NOPE:], rope_b)], axis=-1)
    kv = (rms_norm(ckv, ckv_g) @ w_ukv).reshape(b, s, MLA_HEADS, MLA_NOPE + MLA_V)
    kr = apply_rope(kr.reshape(b, s, 1, MLA_ROPE), rope_b)
    kb = jnp.concatenate([kv[..., :MLA_NOPE], jnp.broadcast_to(kr, (b, s, MLA_HEADS, MLA_ROPE))], axis=-1)
    vb = kv[..., MLA_NOPE:]
    return qa, ka, va, qb, kb, vb, gate


def attn_mixer(h_lat, h_ctx, w_in, qn_g, kn_g, cq_g, ckv_g, w_uq, w_ukv, w_out, need_ctx):
    n = h_lat.shape[1]
    rope_a = axial_rope(n, GQA_HEAD_DIM)
    rope_b = axial_rope(n, MLA_ROPE)
    qa, ka, va, qb, kb, vb, g = attn_prep(h_lat, w_in, qn_g, kn_g, cq_g, ckv_g, w_uq, w_ukv, rope_a, rope_b)
    qa_c, ka_c, va_c, qb_c, kb_c, vb_c, g_c = attn_prep(h_ctx, w_in, qn_g, kn_g, cq_g, ckv_g, w_uq, w_ukv, None, None)
    group = GQA_HEADS // GQA_KV_HEADS

    def mix(qa, qb, ka, va, kb, vb, g):
        b, s = qa.shape[:2]
        oa = block_attention(qa.reshape(b, s, GQA_KV_HEADS, group, GQA_HEAD_DIM), ka, va, GQA_HEAD_DIM ** -0.5)
        ob = block_attention(qb.reshape(b, s, MLA_HEADS, 1, MLA_NOPE + MLA_ROPE), kb, vb,
                             (MLA_NOPE + MLA_ROPE) ** -0.5)
        y = jnp.concatenate([oa, ob], axis=-1) * jax.nn.silu(g)
        return y @ w_out

    out_lat = mix(qa, qb,
                  jnp.concatenate([ka, ka_c], axis=1), jnp.concatenate([va, va_c], axis=1),
                  jnp.concatenate([kb, kb_c], axis=1), jnp.concatenate([vb, vb_c], axis=1), g)
    out_ctx = mix(qa_c, qb_c, ka_c, va_c, kb_c, vb_c, g_c) if need_ctx else None
    return out_lat, out_ctx


def rec_prep(h, w_in, wg_f, bg_f, wg_b, bg_b):
    b, s, _ = h.shape
    q, k, v, gdf, gdb, u, gate = split_cols(h @ w_in, REC_SIZES)
    q = q.reshape(b, s, GLA_HEADS, GLA_DK) * (GLA_DK ** -0.5)
    k = k.reshape(b, s, GLA_HEADS, GLA_DK)
    v = v.reshape(b, s, GLA_HEADS, GLA_DV)

    def log_gate(gd, w, bias):
        z = (gd @ w + bias).astype(F32)
        return (jax.nn.log_sigmoid(z) / GLA_GATE_TAU).reshape(b, s, GLA_HEADS, GLA_DK)

    lf = log_gate(gdf, wg_f, bg_f)
    lb = log_gate(gdb, wg_b, bg_b)
    u = u.reshape(b, s, FNET_GROUPS, FNET_DIM)
    return q, k, v, lf, lb, u, gate


def gla_scan(q, k, v, lg, s0):
    b, s, h, dk = q.shape
    dv = v.shape[-1]
    nc = s // GLA_CHUNK

    def chunks(t):
        return t.reshape(b, nc, GLA_CHUNK, h, t.shape[-1]).transpose(1, 0, 3, 2, 4)

    mask = jnp.tril(jnp.ones((GLA_CHUNK, GLA_CHUNK), dtype=bool))

    def step(state, inp):
        qc, kc, vc, gc = inp
        qf = qc.astype(F32)
        kf = kc.astype(F32)
        vf = vc.astype(F32)
        cum = jnp.cumsum(gc, axis=2)
        o_inter = jnp.einsum('bhcd,bhde->bhce', qf * jnp.exp(cum), state)
        decay = jnp.exp(jnp.where(mask[:, :, None], cum[:, :, :, None, :] - cum[:, :, None, :, :], -jnp.inf))
        att = jnp.einsum('bhid,bhjd,bhijd->bhij', qf, kf, decay)
        o_intra = jnp.einsum('bhij,bhje->bhie', att, vf)
        last = cum[:, :, -1, :]
        state = jnp.exp(last)[..., None] * state + jnp.einsum(
            'bhcd,bhce->bhde', kf * jnp.exp(last[:, :, None, :] - cum), vf)
        return state, o_inter + o_intra

    state, o = lax.scan(step, s0, (chunks(q), chunks(k), chunks(v), chunks(lg)))
    o = o.transpose(1, 0, 3, 2, 4).reshape(b, s, h, dv).astype(v.dtype)
    return o, state


def gla_bidir(q, k, v, lf, lb, sf0, sb0):
    of, sf = gla_scan(q, k, v, lf, sf0)
    ob, sb = gla_scan(jnp.flip(q, 1), jnp.flip(k, 1), jnp.flip(v, 1), jnp.flip(lb, 1), sb0)
    return of + jnp.flip(ob, 1), sf, sb


def fourier_mix(u):
    return jnp.fft.fft2(u.astype(F32), axes=(1, 3), norm='ortho').real.astype(u.dtype)


def rec_mixer(h_lat, h_ctx, w_in, wg_f, bg_f, wg_b, bg_b, on_g, w_out, need_ctx):
    q, k, v, lf, lb, u, g = rec_prep(h_lat, w_in, wg_f, bg_f, wg_b, bg_b)
    qc, kc, vc, lfc, lbc, uc, gc = rec_prep(h_ctx, w_in, wg_f, bg_f, wg_b, bg_b)
    zero = jnp.zeros((h_ctx.shape[0], GLA_HEADS, GLA_DK, GLA_DV), F32)
    o_ctx, s_f, s_b = gla_bidir(qc, kc, vc, lfc, lbc, zero, zero)
    o_lat, _, _ = gla_bidir(q, k, v, lf, lb, s_f, s_b)

    def finish(o, u, g):
        b, s = o.shape[:2]
        o = rms_norm(o, on_g).reshape(b, s, GLA_HEADS * GLA_DV)
        f = fourier_mix(u).reshape(b, s, FNET_GROUPS * FNET_DIM)
        return (jnp.concatenate([o, f], axis=-1) * jax.nn.silu(g)) @ w_out

    out_lat = finish(o_lat, u, g)
    out_ctx = finish(o_ctx, uc, gc) if need_ctx else None
    return out_lat, out_ctx


def setup_inputs(seed: int = 0) -> dict:
    key = jax.random.key(seed)
    ks = iter(jax.random.split(key, 32))
    D = D_MODEL

    def nrm(shape, std):
        return jax.random.normal(next(ks), shape, jnp.float32) * std

    def gain(shape):
        return 1.0 + nrm(shape, 0.02)

    return {
        'x': nrm((BATCH, SEQ, D), 1.0),
        'c': nrm((BATCH, D), 1.0),
        'ctx': nrm((BATCH, CTX_LEN, D), 1.0),
        'c_ctx': nrm((D,), 1.0),
        'norm_g': gain((DEPTH, D)),
        'ada_w': nrm((DEPTH, D, 3 * D), 0.5 * D ** -0.5),
        'ada_b': nrm((DEPTH, 3 * D), 0.02),
        'att_w_in': nrm((N_EVEN, D, ATT_IN), D ** -0.5),
        'att_qn_g': gain((N_EVEN, GQA_HEAD_DIM)),
        'att_kn_g': gain((N_EVEN, GQA_HEAD_DIM)),
        'mla_cq_g': gain((N_EVEN, MLA_Q_LORA)),
        'mla_ckv_g': gain((N_EVEN, MLA_KV_LORA)),
        'mla_w_uq': nrm((N_EVEN, MLA_Q_LORA, MLA_HEADS * (MLA_NOPE + MLA_ROPE)), MLA_Q_LORA ** -0.5),
        'mla_w_ukv': nrm((N_EVEN, MLA_KV_LORA, MLA_HEADS * (MLA_NOPE + MLA_V)), MLA_KV_LORA ** -0.5),
        'att_w_out': nrm((N_EVEN, ATT_MIX, D), ATT_MIX ** -0.5),
        'rec_w_in': nrm((N_ODD, D, REC_IN), D ** -0.5),
        'gla_wg_f': nrm((N_ODD, GLA_GATE_RANK, GLA_HEADS * GLA_DK), GLA_GATE_RANK ** -0.5),
        'gla_bg_f': nrm((N_ODD, GLA_HEADS * GLA_DK), 0.1),
        'gla_wg_b': nrm((N_ODD, GLA_GATE_RANK, GLA_HEADS * GLA_DK), GLA_GATE_RANK ** -0.5),
        'gla_bg_b': nrm((N_ODD, GLA_HEADS * GLA_DK), 0.1),
        'gla_on_g': gain((N_ODD, GLA_DV)),
        'rec_w_out': nrm((N_ODD, REC_MIX, D), REC_MIX ** -0.5),
        'final_g': gain((D,)),
    }


def reference(x, c, ctx, c_ctx, norm_g, ada_w, ada_b, att_w_in, att_qn_g, att_kn_g, mla_cq_g, mla_ckv_g,
              mla_w_uq, mla_w_ukv, att_w_out, rec_w_in, gla_wg_f, gla_bg_f, gla_wg_b, gla_bg_b, gla_on_g,
              rec_w_out, final_g):
    x_ctx = ctx
    for layer in range(DEPTH):
        need_ctx = layer < DEPTH - 1
        mod = jax.nn.silu(c) @ ada_w[layer] + ada_b[layer]
        shift, scale, gate = jnp.split(mod[:, None, :], 3, axis=-1)
        mod_c = jax.nn.silu(c_ctx) @ ada_w[layer] + ada_b[layer]
        shift_c, scale_c, gate_c = jnp.split(mod_c, 3, axis=-1)
        h = rms_norm(x, norm_g[layer]) * (1.0 + scale) + shift
        hc = rms_norm(x_ctx, norm_g[layer]) * (1.0 + scale_c) + shift_c
        i = layer // 2
        if layer % 2 == 0:
            y, yc = attn_mixer(h, hc, att_w_in[i], att_qn_g[i], att_kn_g[i], mla_cq_g[i], mla_ckv_g[i],
                               mla_w_uq[i], mla_w_ukv[i], att_w_out[i], need_ctx)
        else:
            y, yc = rec_mixer(h, hc, rec_w_in[i], gla_wg_f[i], gla_bg_f[i], gla_wg_b[i], gla_bg_b[i],
                              gla_on_g[i], rec_w_out[i], need_ctx)
        x = x + gate * y
        if need_ctx:
            x_ctx = x_ctx + gate_c * yc
    return rms_norm(x, final_g)
```

```python
import functools

import numpy as np
import jax
import jax.numpy as jnp
from jax import lax
from jax.experimental import pallas as pl
from jax.experimental.pallas import tpu as pltpu

F32 = jnp.float32
BF16 = jnp.bfloat16

EPS = 1e-6
GRID_W = 64
ROPE_THETA = 10000.0

GQA_HEADS = 16
GQA_KV_HEADS = 4
GQA_GROUP = GQA_HEADS // GQA_KV_HEADS
GQA_HEAD_DIM = 128
MLA_HEADS = 16
MLA_Q_LORA = 1024
MLA_KV_LORA = 512
MLA_NOPE = 128
MLA_ROPE = 64
MLA_V = 128
MLA_QK_PAD = 256

GLA_HEADS = 6
GLA_DK = 256
GLA_DV = 512
GLA_GATE_RANK = 16
GLA_GATE_TAU = 16.0
GLA_CHUNK = 128
GLA_SUB = 16
FNET_GROUPS = 4
FNET_DIM = 256
FFT_N1 = 64
FFT_T2 = 16

ATT_QA, ATT_KA, ATT_VA, ATT_CQ, ATT_CKV, ATT_KR, ATT_GATE, ATT_IN_PAD = 0, 2048, 2560, 3072, 4096, 4608, 5120, 9216
REC_GATE, REC_Q, REC_K, REC_V, REC_U, REC_IN_MAIN = 0, 4096, 5632, 7168, 10240, 11264

VMEM_LIMIT = 56 * 1024 * 1024


def _cparams(sem):
    return pltpu.CompilerParams(dimension_semantics=sem, vmem_limit_bytes=VMEM_LIMIT)


def _silu(x):
    return x * jax.nn.sigmoid(x)


def _adaln_body(a_ref, w_ref, b_ref, o_ref):
    a = _silu(a_ref[...]).astype(BF16)
    o_ref[0] = jnp.dot(a, w_ref[0].astype(BF16), preferred_element_type=F32) + b_ref[0]


def adaln(cond, ada_w, ada_b):
    depth, d, n = ada_w.shape
    tn = 512
    return pl.pallas_call(
        _adaln_body,
        grid=(depth, n // tn),
        in_specs=[pl.BlockSpec((8, d), lambda l, j: (0, 0)),
                  pl.BlockSpec((1, d, tn), lambda l, j: (l, 0, j)),
                  pl.BlockSpec((1, 1, tn), lambda l, j: (l, 0, j))],
        out_specs=pl.BlockSpec((1, 8, tn), lambda l, j: (l, 0, j)),
        out_shape=jax.ShapeDtypeStruct((depth, 8, n), F32),
        compiler_params=_cparams(("parallel", "parallel")),
        name="adaln",
    )(cond, ada_w, ada_b.reshape(depth, 1, n))


def _modnorm_body(x_ref, g_ref, sc_ref, sh_ref, o_ref):
    x = x_ref[...]
    y = x * lax.rsqrt(jnp.mean(x * x, axis=-1, keepdims=True) + EPS)
    y = y * g_ref[...]
    o_ref[...] = (y * (1.0 + sc_ref[0]) + sh_ref[0]).astype(o_ref.dtype)


def modnorm(x, g, scale, shift, rows_per_group, out_dtype):
    r, d = x.shape
    tm = 256
    ngroups = scale.shape[0]
    tiles_per_group = rows_per_group // tm
    gmap = lambda i: (jnp.minimum(i // tiles_per_group, ngroups - 1), 0, 0)
    return pl.pallas_call(
        _modnorm_body,
        grid=(r // tm,),
        in_specs=[pl.BlockSpec((tm, d), lambda i: (i, 0)),
                  pl.BlockSpec((1, d), lambda i: (0, 0)),
                  pl.BlockSpec((1, 1, d), gmap),
                  pl.BlockSpec((1, 1, d), gmap)],
        out_specs=pl.BlockSpec((tm, d), lambda i: (i, 0)),
        out_shape=jax.ShapeDtypeStruct((r, d), out_dtype),
        compiler_params=_cparams(("parallel",)),
        name="modnorm",
    )(x, g.reshape(1, d), scale.reshape(ngroups, 1, d), shift.reshape(ngroups, 1, d))


def _mm_body(a_ref, b_ref, o_ref):
    o_ref[...] = jnp.dot(a_ref[...], b_ref[...], preferred_element_type=F32).astype(o_ref.dtype)


def matmul(a, b, tm, tn, out_dtype):
    m, k = a.shape
    n = b.shape[1]
    return pl.pallas_call(
        _mm_body,
        grid=(n // tn, m // tm),
        in_specs=[pl.BlockSpec((tm, k), lambda j, i: (i, 0)),
                  pl.BlockSpec((k, tn), lambda j, i: (0, j))],
        out_specs=pl.BlockSpec((tm, tn), lambda j, i: (i, j)),
        out_shape=jax.ShapeDtypeStruct((m, n), out_dtype),
        compiler_params=_cparams(("parallel", "parallel")),
        name="matmul",
    )(a, b)


def _mm_res_body(a1_ref, a2_ref, b_ref, x_ref, g_ref, o_ref):
    k1 = a1_ref.shape[1]
    acc = jnp.dot(a1_ref[...], b_ref[:k1, :], preferred_element_type=F32)
    acc = acc + jnp.dot(a2_ref[...], b_ref[k1:, :], preferred_element_type=F32)
    o_ref[...] = x_ref[...] + g_ref[0] * acc


def matmul_residual(a1, a2, b, x, gate, rows_per_group, tm, tn):
    m, k1 = a1.shape
    k2 = a2.shape[1]
    n = b.shape[1]
    ngroups = gate.shape[0]
    tiles_per_group = rows_per_group // tm
    return pl.pallas_call(
        _mm_res_body,
        grid=(n // tn, m // tm),
        in_specs=[pl.BlockSpec((tm, k1), lambda j, i: (i, 0)),
                  pl.BlockSpec((tm, k2), lambda j, i: (i, 0)),
                  pl.BlockSpec((k1 + k2, tn), lambda j, i: (0, j)),
                  pl.BlockSpec((tm, tn), lambda j, i: (i, j)),
                  pl.BlockSpec((1, 1, tn), lambda j, i: (jnp.minimum(i // tiles_per_group, ngroups - 1), 0, j))],
        out_specs=pl.BlockSpec((tm, tn), lambda j, i: (i, j)),
        out_shape=jax.ShapeDtypeStruct((m, n), F32),
        compiler_params=_cparams(("parallel", "parallel")),
        name="matmul_residual",
    )(a1, a2, b, x, gate.reshape(ngroups, 1, n))


def _rope_half(y, cos2, sin2):
    return y * cos2 + pltpu.roll(y, GQA_HEAD_DIM // 2, 1) * sin2


def _rope_quarter(v, cosb, sb1, sb2):
    return v * cosb + pltpu.roll(v, 96, 1) * sb1 + pltpu.roll(v, 32, 1) * sb2


def _rms(x, g):
    return x * lax.rsqrt(jnp.mean(x * x, axis=-1, keepdims=True) + EPS) * g


def _prep_a_body(qa_ref, ka_ref, cq_ref, ckv_ref, kr_ref, cosa_ref, sina_ref, cosb_ref, sb1_ref, sb2_ref,
                 qn_ref, kn_ref, cqg_ref, ckvg_ref, qg_ref, kg_ref, cqn_ref, ckvn_ref, krp_ref):
    cosa, sina = cosa_ref[...], sina_ref[...]
    hd = GQA_HEAD_DIM
    q_scale = hd ** -0.5
    for h in range(GQA_HEADS):
        y = _rms(qa_ref[:, h * hd:(h + 1) * hd].astype(F32), qn_ref[...])
        qg_ref[:, h * hd:(h + 1) * hd] = (_rope_half(y, cosa, sina) * q_scale).astype(BF16)
    for h in range(GQA_KV_HEADS):
        y = _rms(ka_ref[:, h * hd:(h + 1) * hd].astype(F32), kn_ref[...])
        kg_ref[:, h * hd:(h + 1) * hd] = _rope_half(y, cosa, sina).astype(BF16)
    cqn_ref[...] = _rms(cq_ref[...].astype(F32), cqg_ref[...]).astype(BF16)
    ckvn_ref[...] = _rms(ckv_ref[...].astype(F32), ckvg_ref[...]).astype(BF16)
    krp_ref[...] = _rope_quarter(kr_ref[...].astype(F32), cosb_ref[...], sb1_ref[...], sb2_ref[...]).astype(BF16)


def attn_prep(p, tabs, tab_rows, qn_g, kn_g, cq_g, ckv_g):
    r = p.shape[0]
    tm = 256
    tab_tiles = tab_rows // tm
    rowmap = lambda c: (lambda i: (i, c))
    tabmap = lambda i: (i % tab_tiles, 0)
    vec = lambda n: pl.BlockSpec((1, n), lambda i: (0, 0))
    outs = pl.pallas_call(
        _prep_a_body,
        grid=(r // tm,),
        in_specs=[pl.BlockSpec((tm, 2048), rowmap(ATT_QA // 2048)),
                  pl.BlockSpec((tm, 512), rowmap(ATT_KA // 512)),
                  pl.BlockSpec((tm, 1024), rowmap(ATT_CQ // 1024)),
                  pl.BlockSpec((tm, 512), rowmap(ATT_CKV // 512)),
                  pl.BlockSpec((tm, 128), rowmap(ATT_KR // 128)),
                  ] + [pl.BlockSpec((tm, 128), tabmap)] * 5 + [vec(128), vec(128), vec(1024), vec(512)],
        out_specs=[pl.BlockSpec((tm, 2048), lambda i: (i, 0)),
                   pl.BlockSpec((tm, 512), lambda i: (i, 0)),
                   pl.BlockSpec((tm, 1024), lambda i: (i, 0)),
                   pl.BlockSpec((tm, 512), lambda i: (i, 0)),
                   pl.BlockSpec((tm, 128), lambda i: (i, 0))],
        out_shape=[jax.ShapeDtypeStruct((r, 2048), BF16), jax.ShapeDtypeStruct((r, 512), BF16),
                   jax.ShapeDtypeStruct((r, 1024), BF16), jax.ShapeDtypeStruct((r, 512), BF16),
                   jax.ShapeDtypeStruct((r, 128), BF16)],
        compiler_params=_cparams(("parallel",)),
        name="attn_prep",
    )(p, p, p, p, p, *tabs, qn_g.reshape(1, -1), kn_g.reshape(1, -1), cq_g.reshape(1, -1), ckv_g.reshape(1, -1))
    return outs


def _mlaq_body(a_ref, b_ref, cosb_ref, sb1_ref, sb2_ref, o_ref):
    acc = jnp.dot(a_ref[...], b_ref[...], preferred_element_type=F32)
    scale = (MLA_NOPE + MLA_ROPE) ** -0.5
    cosb, sb1, sb2 = cosb_ref[...], sb1_ref[...], sb2_ref[...]
    for h in range(acc.shape[1] // MLA_QK_PAD):
        c0 = h * MLA_QK_PAD
        o_ref[:, c0:c0 + 128] = (acc[:, c0:c0 + 128] * scale).astype(BF16)
        o_ref[:, c0 + 128:c0 + 256] = (_rope_quarter(acc[:, c0 + 128:c0 + 256], cosb, sb1, sb2) * scale).astype(BF16)


def mla_q(cqn, w_uq_pad, tabs_b, tab_rows, tm):
    m, k = cqn.shape
    n = w_uq_pad.shape[1]
    tn = 1024
    tab_tiles = tab_rows // tm
    return pl.pallas_call(
        _mlaq_body,
        grid=(n // tn, m // tm),
        in_specs=[pl.BlockSpec((tm, k), lambda j, i: (i, 0)),
                  pl.BlockSpec((k, tn), lambda j, i: (0, j))] +
                 [pl.BlockSpec((tm, 128), lambda j, i: (i % tab_tiles, 0))] * 3,
        out_specs=pl.BlockSpec((tm, tn), lambda j, i: (i, j)),
        out_shape=jax.ShapeDtypeStruct((m, n), BF16),
        compiler_params=_cparams(("parallel", "parallel")),
        name="mla_q",
    )(cqn, w_uq_pad, *tabs_b)


def _flash_body(*refs, has_lat, kparts, group, tq, tk, s_lat):
    idx = 1
    q_ref = refs[0]
    if has_lat:
        klat = refs[idx:idx + kparts]
        vlat = refs[idx + kparts]
        idx += kparts + 1
    kctx = refs[idx:idx + kparts]
    vctx = refs[idx + kparts]
    idx += kparts + 1
    g_ref, o_ref, m_ref, l_ref, acc_ref = refs[idx:idx + 5]

    dq = q_ref.shape[1] // group
    if group > 1:
        q = jnp.concatenate([q_ref[:, g * dq:(g + 1) * dq] for g in range(group)], axis=0)
    else:
        q = q_ref[...]
    m_ref[...] = jnp.full(m_ref.shape, -1e30, F32)
    l_ref[...] = jnp.zeros(l_ref.shape, F32)
    acc_ref[...] = jnp.zeros(acc_ref.shape, F32)

    def step(kc, vc):
        s = lax.dot_general(q, kc, (((1,), (1,)), ((), ())), preferred_element_type=F32)
        m_prev = m_ref[...]
        m_new = jnp.maximum(m_prev, jnp.max(s, axis=-1, keepdims=True))
        alpha = jnp.exp(m_prev - m_new)
        p = jnp.exp(s - m_new)
        l_ref[...] = alpha * l_ref[...] + jnp.sum(p, axis=-1, keepdims=True)
        acc_ref[...] = alpha * acc_ref[...] + jnp.dot(p.astype(BF16), vc, preferred_element_type=F32)
        m_ref[...] = m_new

    def cat(parts):
        return parts[0] if len(parts) == 1 else jnp.concatenate(parts, axis=1)

    if has_lat:
        def loop_body(c, carry):
            off = pl.multiple_of(c * tk, tk)
            step(cat([r[pl.ds(off, tk), :] for r in klat]), vlat[pl.ds(off, tk), :])
            return carry
        lax.fori_loop(0, s_lat // tk, loop_body, 0)
    step(cat([r[...] for r in kctx]), vctx[...])

    o = acc_ref[...] / l_ref[...]
    dv = acc_ref.shape[1]
    for g in range(group):
        gt = g_ref[:, g * dv:(g + 1) * dv].astype(F32)
        o_ref[:, g * dv:(g + 1) * dv] = (o[g * tq:(g + 1) * tq] * _silu(gt)).astype(BF16)


def flash(q, klat, vlat, kctx, vctx, gate, *, batch, heads, group, dqk, s_lat, s_ctx, tq, q_is_ctx):
    dv = 128
    kparts = len(kctx)
    has_lat = not q_is_ctx
    nq = (s_ctx if q_is_ctx else s_lat) // tq
    tk = 512
    qmap = lambda b, h, i: (b * nq + i, h)
    in_specs = [pl.BlockSpec((tq, group * dqk), qmap)]
    args = [q]
    if has_lat:
        for arr, c0 in klat:
            in_specs.append(pl.BlockSpec((s_lat, 128), functools.partial(lambda b, h, i, c0, per: (b, c0 + per * h), c0=c0[0], per=c0[1])))
            args.append(arr)
        in_specs.append(pl.BlockSpec((s_lat, dv), lambda b, h, i: (b, vlat[1] + h)))
        args.append(vlat[0])
    for arr, c0 in kctx:
        in_specs.append(pl.BlockSpec((s_ctx, 128), functools.partial(lambda b, h, i, c0, per: (b, c0 + per * h), c0=c0[0], per=c0[1])))
        args.append(arr)
    in_specs.append(pl.BlockSpec((s_ctx, dv), lambda b, h, i: (b, vctx[1] + h)))
    args.append(vctx[0])
    in_specs.append(pl.BlockSpec((tq, group * dv), lambda b, h, i: (b * nq + i, gate[1] + h)))
    args.append(gate[0])
    rows = q.shape[0]
    body = functools.partial(_flash_body, has_lat=has_lat, kparts=kparts, group=group, tq=tq, tk=tk, s_lat=s_lat)
    return pl.pallas_call(
        body,
        grid=(batch, heads, nq),
        in_specs=in_specs,
        out_specs=pl.BlockSpec((tq, group * dv), qmap),
        out_shape=jax.ShapeDtypeStruct((rows, heads * group * dv), BF16),
        scratch_shapes=[pltpu.VMEM((group * tq, 1), F32), pltpu.VMEM((group * tq, 1), F32),
                        pltpu.VMEM((group * tq, dv), F32)],
        compiler_params=_cparams(("parallel", "parallel", "parallel")),
        name="flash_ctx" if q_is_ctx else "flash_lat",
    )(*args)


def _gla_body(*refs, rev, with_out, has_init, nsteps):
    c = GLA_CHUNK
    idx = 0
    if with_out:
        q_ref = refs[0]
        idx = 1
    k_ref, v_ref, gd_ref, wg_ref, bg_ref = refs[idx:idx + 5]
    idx += 5
    if has_init:
        s0_ref = refs[idx]
        idx += 1
    out_ref = refs[idx]
    st_ref = refs[idx + 1]
    oacc_ref = refs[idx + 2] if with_out else None

    step = pl.program_id(2)

    @pl.when(step == 0)
    def _():
        if has_init:
            st_ref[...] = s0_ref[0, 0]
        else:
            st_ref[...] = jnp.zeros(st_ref.shape, F32)

    v = v_ref[...]
    kf = k_ref[...].astype(F32)
    z = jnp.dot(gd_ref[...].astype(BF16), wg_ref[0], preferred_element_type=F32) + bg_ref[0]
    lg = (jnp.minimum(z, 0.0) - jnp.log(1.0 + jnp.exp(-jnp.abs(z)))) * (1.0 / GLA_GATE_TAU)
    row = lax.broadcasted_iota(jnp.int32, (c, c), 0)
    col = lax.broadcasted_iota(jnp.int32, (c, c), 1)
    tri = jnp.where((row <= col) if rev else (row >= col), 1.0, 0.0).astype(F32)
    cum = jnp.dot(tri, lg, preferred_element_type=F32, precision=lax.Precision.HIGHEST)
    tot = cum[0:1] if rev else cum[c - 1:c]
    st = st_ref[...]

    if with_out:
        qf = q_ref[...].astype(F32) * (GLA_DK ** -0.5)
        qe = (qf * jnp.exp(cum)).astype(BF16)
        oacc_ref[...] = lax.dot_general(qe, st.astype(BF16), (((1,), (1,)), ((), ())), preferred_element_type=F32)
        blk = c // 2
        while blk >= GLA_SUB:
            for a in range(0, c, 2 * blk):
                if rev:
                    i0, j0, cref = a, a + blk, cum[a + blk:a + blk + 1]
                else:
                    i0, j0, cref = a + blk, a, cum[a + blk - 1:a + blk]
                qi = (qf[i0:i0 + blk] * jnp.exp(cum[i0:i0 + blk] - cref)).astype(BF16)
                kj = (kf[j0:j0 + blk] * jnp.exp(cref - cum[j0:j0 + blk])).astype(BF16)
                att = lax.dot_general(qi, kj, (((1,), (1,)), ((), ())), preferred_element_type=F32)
                oacc_ref[i0:i0 + blk, :] += jnp.dot(att.astype(BF16), v[j0:j0 + blk], preferred_element_type=F32)
            blk //= 2
        sub = GLA_SUB
        ii = lax.broadcasted_iota(jnp.int32, (sub, sub), 0)
        jj = lax.broadcasted_iota(jnp.int32, (sub, sub), 1)
        valid = (jj >= ii) if rev else (jj <= ii)
        for a in range(0, c, sub):
            qs, ks, cs = qf[a:a + sub], kf[a:a + sub], cum[a:a + sub]
            att = jnp.zeros((sub, sub), F32)
            for j in range(sub):
                e = jnp.exp(jnp.minimum(cs - cs[j:j + 1], 0.0))
                colv = jnp.sum(qs * e * ks[j:j + 1], axis=-1, keepdims=True)
                att = att + colv * jnp.where(jj[0:1] == j, 1.0, 0.0)
            att = jnp.where(valid, att, 0.0)
            oacc_ref[a:a + sub, :] += jnp.dot(att.astype(BF16), v[a:a + sub], preferred_element_type=F32)
        out_ref[...] = oacc_ref[...].astype(out_ref.dtype)

    kt = (kf * jnp.exp(tot - cum)).astype(BF16)
    st_new = st * jnp.exp(tot) + lax.dot_general(v, kt, (((0,), (0,)), ((), ())), preferred_element_type=F32)
    st_ref[...] = st_new

    if not with_out:
        @pl.when(step == nsteps - 1)
        def _():
            out_ref[0, 0] = st_new


def gla(q, k, v, gd, wg_pad, bg, s0, *, batch, rows_per_batch, rev):
    c = GLA_CHUNK
    nsteps = rows_per_batch // c
    with_out = q is not None
    has_init = s0 is not None

    def rmap(off):
        if rev:
            return lambda b, h, s: (b * nsteps + nsteps - 1 - s, off + h)
        return lambda b, h, s: (b * nsteps + s, off + h)

    if rev:
        gdmap = lambda b, h, s: (b * nsteps + nsteps - 1 - s, gd[1])
    else:
        gdmap = lambda b, h, s: (b * nsteps + s, gd[1])
    in_specs, args = [], []
    if with_out:
        in_specs.append(pl.BlockSpec((c, GLA_DK), rmap(q[1])))
        args.append(q[0])
    in_specs += [pl.BlockSpec((c, GLA_DK), rmap(k[1])),
                 pl.BlockSpec((c, GLA_DV), rmap(v[1])),
                 pl.BlockSpec((c, 128), gdmap),
                 pl.BlockSpec((1, 128, GLA_DK), lambda b, h, s: (h, 0, 0)),
                 pl.BlockSpec((1, 1, GLA_DK), lambda b, h, s: (h, 0, 0))]
    args += [k[0], v[0], gd[0], wg_pad, bg]
    if has_init:
        in_specs.append(pl.BlockSpec((1, 1, GLA_DV, GLA_DK), lambda b, h, s: (b, h, 0, 0)))
        args.append(s0)
    scratch = [pltpu.VMEM((GLA_DV, GLA_DK), F32)]
    if with_out:
        out_specs = pl.BlockSpec((c, GLA_DV), rmap(0))
        out_shape = jax.ShapeDtypeStruct((batch * rows_per_batch, GLA_HEADS * GLA_DV), BF16)
        scratch.append(pltpu.VMEM((c, GLA_DV), F32))
    else:
        out_specs = pl.BlockSpec((1, 1, GLA_DV, GLA_DK), lambda b, h, s: (b, h, 0, 0))
        out_shape = jax.ShapeDtypeStruct((batch, GLA_HEADS, GLA_DV, GLA_DK), F32)
    body = functools.partial(_gla_body, rev=rev, with_out=with_out, has_init=has_init, nsteps=nsteps)
    return pl.pallas_call(
        body,
        grid=(batch, GLA_HEADS, nsteps),
        in_specs=in_specs,
        out_specs=out_specs,
        out_shape=out_shape,
        scratch_shapes=scratch,
        compiler_params=_cparams(("parallel", "parallel", "arbitrary")),
        name="gla_scan" if with_out else "gla_state",
    )(*args)


def _fft_a_body(x_ref, gc_ref, gs_ref, tr_ref, ti_ref, o_ref):
    n1, t2, ch = x_ref.shape
    x = x_ref[...].reshape(n1 * t2, ch)
    yr = jnp.dot(gc_ref[...], x, preferred_element_type=F32)
    yi = -jnp.dot(gs_ref[...], x, preferred_element_type=F32)
    tr, ti = tr_ref[...], ti_ref[...]
    o_ref[:, :, :ch] = (yr * tr - yi * ti).astype(BF16).reshape(n1, t2, ch)
    o_ref[:, :, ch:] = (yr * ti + yi * tr).astype(BF16).reshape(n1, t2, ch)


def _fft_b_body(y_ref, fc_ref, fs_ref, cc_ref, sc_ref, o_ref, *, norm):
    ch = y_ref.shape[1] // 2
    yr, yi = y_ref[:, :ch], y_ref[:, ch:]
    fc, fs = fc_ref[...], fs_ref[...]
    zr = jnp.dot(fc, yr, preferred_element_type=F32) + jnp.dot(fs, yi, preferred_element_type=F32)
    zi = jnp.dot(fc, yi, preferred_element_type=F32) - jnp.dot(fs, yr, preferred_element_type=F32)
    for g in range(FNET_GROUPS):
        sl = slice(g * FNET_DIM, (g + 1) * FNET_DIM)
        f = jnp.dot(zr[:, sl].astype(BF16), cc_ref[...], preferred_element_type=F32)
        f = f + jnp.dot(zi[:, sl].astype(BF16), sc_ref[...], preferred_element_type=F32)
        o_ref[:, sl] = (f * norm).astype(BF16)


def _dft_tables(s):
    n1, n2, t2 = FFT_N1, s // FFT_N1, FFT_T2
    a = np.arange(n1)
    ang1 = 2.0 * np.pi * ((a[:, None] * a[None, :]) % n1) / n1
    eye = np.eye(t2)
    gc = np.kron(np.cos(ang1), eye)
    gs = np.kron(np.sin(ang1), eye)
    b = np.arange(n2)
    angt = 2.0 * np.pi * ((a[:, None] * b[None, :]) % s) / s
    tw = angt.reshape(n1, n2 // t2, t2).transpose(1, 0, 2).reshape(n2 // t2, n1 * t2, 1)
    ang2 = 2.0 * np.pi * ((b[:, None] * b[None, :]) % n2) / n2
    d = np.arange(FNET_DIM)
    angc = 2.0 * np.pi * ((d[:, None] * d[None, :]) % FNET_DIM) / FNET_DIM
    bf = lambda m: jnp.asarray(m, dtype=BF16)
    return dict(gc=bf(gc), gs=bf(gs), tr=jnp.asarray(np.cos(tw), F32), ti=jnp.asarray(-np.sin(tw), F32),
                fc=bf(np.cos(ang2)), fs=bf(np.sin(ang2)), cc=bf(np.cos(angc)), sc=bf(np.sin(angc)))


def fourier_mix(p, u_col, batch, s):
    n1, n2, t2 = FFT_N1, s // FFT_N1, FFT_T2
    ch = FNET_GROUPS * FNET_DIM
    tabs = _dft_tables(s)
    p4 = p.reshape(batch, n1, n2, p.shape[1])
    full = lambda shape: pl.BlockSpec(shape, lambda b, j: (0,) * len(shape))
    y = pl.pallas_call(
        _fft_a_body,
        grid=(batch, n2 // t2),
        in_specs=[pl.BlockSpec((None, n1, t2, ch), lambda b, j: (b, 0, j, u_col // ch)),
                  full((n1 * t2, n1 * t2)), full((n1 * t2, n1 * t2)),
                  pl.BlockSpec((None, n1 * t2, 1), lambda b, j: (j, 0, 0)),
                  pl.BlockSpec((None, n1 * t2, 1), lambda b, j: (j, 0, 0))],
        out_specs=pl.BlockSpec((None, n1, t2, 2 * ch), lambda b, j: (b, 0, j, 0)),
        out_shape=jax.ShapeDtypeStruct((batch, n1, n2, 2 * ch), BF16),
        compiler_params=_cparams(("parallel", "parallel")),
        name="fft_stage_a",
    )(p4, tabs["gc"], tabs["gs"], tabs["tr"], tabs["ti"])
    norm = float(1.0 / np.sqrt(s * FNET_DIM))
    f = pl.pallas_call(
        functools.partial(_fft_b_body, norm=norm),
        grid=(batch, n1),
        in_specs=[pl.BlockSpec((None, None, n2, 2 * ch), lambda b, j: (b, j, 0, 0)),
                  full((n2, n2)), full((n2, n2)), full((FNET_DIM, FNET_DIM)), full((FNET_DIM, FNET_DIM))],
        out_specs=pl.BlockSpec((None, None, n2, ch), lambda b, j: (b, j, 0, 0)),
        out_shape=jax.ShapeDtypeStruct((batch, n1, n2, ch), BF16),
        compiler_params=_cparams(("parallel", "parallel")),
        name="fft_stage_b",
    )(y, tabs["fc"], tabs["fs"], tabs["cc"], tabs["sc"])
    return f.transpose(0, 2, 1, 3).reshape(batch * s, ch)


def _finish_body(of_ref, ob_ref, f_ref, g_ref, ong_ref, o_ref):
    dv = GLA_DV
    for h in range(GLA_HEADS):
        sl = slice(h * dv, (h + 1) * dv)
        o = of_ref[:, sl].astype(F32) + ob_ref[:, sl].astype(F32)
        y = _rms(o, ong_ref[...])
        gt = g_ref[:, sl].astype(F32)
        o_ref[:, sl] = (y * _silu(gt)).astype(BF16)
    base = GLA_HEADS * dv
    gt = g_ref[:, base:].astype(F32)
    o_ref[:, base:] = (f_ref[...].astype(F32) * _silu(gt)).astype(BF16)


def rec_finish(o_f, o_b, f, p, on_g):
    r = o_f.shape[0]
    tm = 256
    w = GLA_HEADS * GLA_DV
    mix = w + FNET_GROUPS * FNET_DIM
    return pl.pallas_call(
        _finish_body,
        grid=(r // tm,),
        in_specs=[pl.BlockSpec((tm, w), lambda i: (i, 0)),
                  pl.BlockSpec((tm, w), lambda i: (i, 0)),
                  pl.BlockSpec((tm, mix - w), lambda i: (i, 0)),
                  pl.BlockSpec((tm, mix), lambda i: (i, REC_GATE // mix)),
                  pl.BlockSpec((1, GLA_DV), lambda i: (0, 0))],
        out_specs=pl.BlockSpec((tm, mix), lambda i: (i, 0)),
        out_shape=jax.ShapeDtypeStruct((r, mix), BF16),
        compiler_params=_cparams(("parallel",)),
        name="rec_finish",
    )(o_f, o_b, f, p, on_g.reshape(1, -1))


def _rope_tables(s):
    t = np.arange(s)
    row, col = (t // GRID_W).astype(np.float32), (t % GRID_W).astype(np.float32)

    def cs(rot_dim):
        quarter = rot_dim // 4
        inv_freq = (np.float32(ROPE_THETA) ** (-np.arange(quarter, dtype=np.float32) / quarter)).astype(np.float32)
        ang = np.concatenate([row[:, None] * inv_freq, col[:, None] * inv_freq], axis=-1).astype(np.float32)
        return np.cos(ang), np.sin(ang)

    ca, sa = cs(GQA_HEAD_DIM)
    cb, sb = cs(MLA_ROPE)
    z32, z64 = np.zeros((s, 32), np.float32), np.zeros((s, 64), np.float32)
    tabs = [np.concatenate([ca, ca], 1), np.concatenate([-sa, sa], 1),
            np.concatenate([cb, cb, z64], 1), np.concatenate([-sb, z32, z64], 1), np.concatenate([z32, sb, z64], 1)]
    return [jnp.asarray(t_, F32) for t_ in tabs]


def _identity_rope_tables(n):
    one, zero = np.ones((n, 128), np.float32), np.zeros((n, 128), np.float32)
    cb = np.concatenate([np.ones((n, 64), np.float32), np.zeros((n, 64), np.float32)], 1)
    return [jnp.asarray(t_, F32) for t_ in (one, zero, cb, zero, zero)]


def _attention_layer(x_lat, x_ctx, mod, norm_g, w_in, qn_g, kn_g, cq_g, ckv_g, w_uq, w_ukv, w_out, batch, s, n_ctx):
    d = x_lat.shape[1]
    shift, scale, gate = mod[:, :d], mod[:, d:2 * d], mod[:, 2 * d:]
    h_lat = modnorm(x_lat, norm_g, scale[:batch], shift[:batch], s, BF16)
    h_ctx = modnorm(x_ctx, norm_g, scale[batch:batch + 1], shift[batch:batch + 1], batch * n_ctx, BF16)

    w_in_p = jnp.concatenate([w_in[:, :4672], jnp.zeros((d, ATT_GATE - 4672), w_in.dtype), w_in[:, 4672:]], axis=1).astype(BF16)
    w_uq_p = jnp.pad(w_uq.reshape(MLA_Q_LORA, MLA_HEADS, MLA_NOPE + MLA_ROPE),
                     ((0, 0), (0, 0), (0, MLA_QK_PAD - MLA_NOPE - MLA_ROPE))).reshape(MLA_Q_LORA, MLA_HEADS * MLA_QK_PAD).astype(BF16)
    w_ukv3 = w_ukv.reshape(MLA_KV_LORA, MLA_HEADS, MLA_NOPE + MLA_V)
    w_ukv_p = jnp.concatenate([w_ukv3[:, :, :MLA_NOPE].reshape(MLA_KV_LORA, -1),
                               w_ukv3[:, :, MLA_NOPE:].reshape(MLA_KV_LORA, -1)], axis=1).astype(BF16)
    w_out_b = w_out.astype(BF16)

    ctx_rows = batch * n_ctx
    p_lat = matmul(h_lat, w_in_p, 1024, 1024, BF16)
    p_ctx = matmul(h_ctx, w_in_p, ctx_rows, 1024, BF16)

    tabs_lat, tabs_ctx = _rope_tables(s), _identity_rope_tables(ctx_rows)
    qg_l, kg_l, cqn_l, ckvn_l, krp_l = attn_prep(p_lat, tabs_lat, s, qn_g, kn_g, cq_g, ckv_g)
    qg_c, kg_c, cqn_c, ckvn_c, krp_c = attn_prep(p_ctx, tabs_ctx, ctx_rows, qn_g, kn_g, cq_g, ckv_g)
    qm_l = mla_q(cqn_l, w_uq_p, tabs_lat[2:], s, 1024)
    qm_c = mla_q(cqn_c, w_uq_p, tabs_ctx[2:], ctx_rows, ctx_rows)
    kv_l = matmul(ckvn_l, w_ukv_p, 1024, 1024, BF16)
    kv_c = matmul(ckvn_c, w_ukv_p, ctx_rows, 1024, BF16)

    va_col, gate_a_col, gate_b_col = ATT_VA // 128, ATT_GATE // 512, (ATT_GATE + 2048) // 128
    common = dict(batch=batch, s_lat=s, s_ctx=n_ctx)
    ya_l = flash(qg_l, [(kg_l, (0, 1))], (p_lat, va_col), [(kg_c, (0, 1))], (p_ctx, va_col), (p_lat, gate_a_col),
                 heads=GQA_KV_HEADS, group=GQA_GROUP, dqk=GQA_HEAD_DIM, tq=256, q_is_ctx=False, **common)
    ya_c = flash(qg_c, None, None, [(kg_c, (0, 1))], (p_ctx, va_col), (p_ctx, gate_a_col),
                 heads=GQA_KV_HEADS, group=GQA_GROUP, dqk=GQA_HEAD_DIM, tq=n_ctx, q_is_ctx=True, **common)
    yb_l = flash(qm_l, [(kv_l, (0, 1)), (krp_l, (0, 0))], (kv_l, MLA_HEADS), [(kv_c, (0, 1)), (krp_c, (0, 0))],
                 (kv_c, MLA_HEADS), (p_lat, gate_b_col),
                 heads=MLA_HEADS, group=1, dqk=MLA_QK_PAD, tq=1024, q_is_ctx=False, **common)
    yb_c = flash(qm_c, None, None, [(kv_c, (0, 1)), (krp_c, (0, 0))], (kv_c, MLA_HEADS), (p_ctx, gate_b_col),
                 heads=MLA_HEADS, group=1, dqk=MLA_QK_PAD, tq=n_ctx, q_is_ctx=True, **common)

    x_lat = matmul_residual(ya_l, yb_l, w_out_b, x_lat, gate[:batch], s, 1024, 512)
    x_ctx = matmul_residual(ya_c, yb_c, w_out_b, x_ctx, gate[batch:batch + 1], ctx_rows, ctx_rows, 512)
    return x_lat, x_ctx


def _recurrent_layer(x_lat, x_ctx, mod, norm_g, w_in, wg_f, bg_f, wg_b, bg_b, on_g, w_out, batch, s, n_ctx):
    d = x_lat.shape[1]
    shift, scale, gate = mod[:, :d], mod[:, d:2 * d], mod[:, 2 * d:]
    h_lat = modnorm(x_lat, norm_g, scale[:batch], shift[:batch], s, BF16)
    h_ctx = modnorm(x_ctx, norm_g, scale[batch:batch + 1], shift[batch:batch + 1], batch * n_ctx, BF16)

    w_main = jnp.concatenate([w_in[:, 7200:], w_in[:, :6144], w_in[:, 6176:7200]], axis=1).astype(BF16)
    w_gd = jnp.pad(w_in[:, 6144:6176], ((0, 0), (0, 96))).astype(BF16)
    w_out_b = w_out.astype(BF16)

    def gate_w(wg, row0):
        w3 = wg.reshape(GLA_GATE_RANK, GLA_HEADS, GLA_DK).transpose(1, 0, 2)
        return jnp.pad(w3, ((0, 0), (row0, 128 - GLA_GATE_RANK - row0), (0, 0))).astype(BF16)

    wgf_p, wgb_p = gate_w(wg_f, 0), gate_w(wg_b, GLA_GATE_RANK)
    bgf, bgb = bg_f.reshape(GLA_HEADS, 1, GLA_DK), bg_b.reshape(GLA_HEADS, 1, GLA_DK)

    ctx_rows = batch * n_ctx
    p_lat = matmul(h_lat, w_main, 1024, 1024, BF16)
    p_ctx = matmul(h_ctx, w_main, ctx_rows, 1024, BF16)
    gd_lat = matmul(h_lat, w_gd, 1024, 128, F32)
    gd_ctx = matmul(h_ctx, w_gd, ctx_rows, 128, F32)

    kq, kk, kv = REC_Q // GLA_DK, REC_K // GLA_DK, REC_V // GLA_DV
    sf = gla(None, (p_ctx, kk), (p_ctx, kv), (gd_ctx, 0), wgf_p, bgf, None, batch=batch, rows_per_batch=n_ctx, rev=False)
    sb = gla(None, (p_ctx, kk), (p_ctx, kv), (gd_ctx, 0), wgb_p, bgb, None, batch=batch, rows_per_batch=n_ctx, rev=True)
    o_f = gla((p_lat, kq), (p_lat, kk), (p_lat, kv), (gd_lat, 0), wgf_p, bgf, sf, batch=batch, rows_per_batch=s, rev=False)
    o_b = gla((p_lat, kq), (p_lat, kk), (p_lat, kv), (gd_lat, 0), wgb_p, bgb, sb, batch=batch, rows_per_batch=s, rev=True)

    f = fourier_mix(p_lat, REC_U, batch, s)
    y = rec_finish(o_f, o_b, f, p_lat, on_g)
    return matmul_residual_single(y, w_out_b, x_lat, gate[:batch], s, 1024, 512)


def _mm_res1_body(a_ref, b_ref, x_ref, g_ref, o_ref):
    acc = jnp.dot(a_ref[...], b_ref[...], preferred_element_type=F32)
    o_ref[...] = x_ref[...] + g_ref[0] * acc


def matmul_residual_single(a, b, x, gate, rows_per_group, tm, tn):
    m, k = a.shape
    n = b.shape[1]
    ngroups = gate.shape[0]
    tiles_per_group = rows_per_group // tm
    return pl.pallas_call(
        _mm_res1_body,
        grid=(n // tn, m // tm),
        in_specs=[pl.BlockSpec((tm, k), lambda j, i: (i, 0)),
                  pl.BlockSpec((k, tn), lambda j, i: (0, j)),
                  pl.BlockSpec((tm, tn), lambda j, i: (i, j)),
                  pl.BlockSpec((1, 1, tn), lambda j, i: (jnp.minimum(i // tiles_per_group, ngroups - 1), 0, j))],
        out_specs=pl.BlockSpec((tm, tn), lambda j, i: (i, j)),
        out_shape=jax.ShapeDtypeStruct((m, n), F32),
        compiler_params=_cparams(("parallel", "parallel")),
        name="matmul_residual1",
    )(a, b, x, gate.reshape(ngroups, 1, n))


def kernel(x, c, ctx, c_ctx, norm_g, ada_w, ada_b, att_w_in, att_qn_g, att_kn_g, mla_cq_g, mla_ckv_g, mla_w_uq,
           mla_w_ukv, att_w_out, rec_w_in, gla_wg_f, gla_bg_f, gla_wg_b, gla_bg_b, gla_on_g, rec_w_out, final_g):
    batch, s, d = x.shape
    n_ctx = ctx.shape[1]
    x_lat = x.reshape(batch * s, d)
    x_ctx = ctx.reshape(batch * n_ctx, d)
    cond = jnp.concatenate([c, c_ctx[None, :], jnp.zeros((8 - batch - 1, d), F32)], axis=0)
    mods = adaln(cond, ada_w, ada_b)

    x_lat, x_ctx = _attention_layer(x_lat, x_ctx, mods[0], norm_g[0], att_w_in[0], att_qn_g[0], att_kn_g[0],
                                    mla_cq_g[0], mla_ckv_g[0], mla_w_uq[0], mla_w_ukv[0], att_w_out[0],
                                    batch, s, n_ctx)
    x_lat = _recurrent_layer(x_lat, x_ctx, mods[1], norm_g[1], rec_w_in[0], gla_wg_f[0], gla_bg_f[0], gla_wg_b[0],
                             gla_bg_b[0], gla_on_g[0], rec_w_out[0], batch, s, n_ctx)
    zero = jnp.zeros((1, d), F32)
    out = modnorm(x_lat, final_g, zero, zero, batch * s, F32)
    return out.reshape(batch, s, d)
```

```python
import functools

import numpy as np
import jax
import jax.numpy as jnp
from jax import lax
from jax.experimental import pallas as pl
from jax.experimental.pallas import tpu as pltpu

F32 = jnp.float32
BF16 = jnp.bfloat16

EPS = 1e-6
GRID_W = 64
ROPE_THETA = 10000.0

GQA_HEADS = 16
GQA_KV_HEADS = 4
GQA_GROUP = GQA_HEADS // GQA_KV_HEADS
GQA_HEAD_DIM = 128
MLA_HEADS = 16
MLA_Q_LORA = 1024
MLA_KV_LORA = 512
MLA_NOPE = 128
MLA_ROPE = 64
MLA_V = 128
MLA_QK_PAD = 256

GLA_HEADS = 6
GLA_DK = 256
GLA_DV = 512
GLA_GATE_RANK = 16
GLA_GATE_TAU = 16.0
GLA_CHUNK = 128
GLA_SUB = 16
FNET_GROUPS = 4
FNET_DIM = 256
FLASH_TN = 512
FLASH_TK = 512
LOG2E = 1.4426950408889634
FFT_N1 = 64
FFT_T2 = 16

ATT_QA, ATT_KA, ATT_VA, ATT_CQ, ATT_CKV, ATT_KR, ATT_GATE, ATT_IN_PAD = 0, 2048, 2560, 3072, 4096, 4608, 5120, 9216
REC_GATE, REC_Q, REC_K, REC_V, REC_U, REC_IN_MAIN = 0, 4096, 5632, 7168, 10240, 11264

VMEM_LIMIT = 56 * 1024 * 1024


def _cparams(sem):
    return pltpu.CompilerParams(dimension_semantics=sem, vmem_limit_bytes=VMEM_LIMIT)


def _silu(x):
    return x * jax.nn.sigmoid(x)


def _adaln_body(a_ref, w_ref, b_ref, o_ref):
    a = _silu(a_ref[...]).astype(BF16)
    o_ref[0] = jnp.dot(a, w_ref[0].astype(BF16), preferred_element_type=F32) + b_ref[0]


def adaln(cond, ada_w, ada_b):
    depth, d, n = ada_w.shape
    tn = 512
    return pl.pallas_call(
        _adaln_body,
        grid=(depth, n // tn),
        in_specs=[pl.BlockSpec((8, d), lambda l, j: (0, 0)),
                  pl.BlockSpec((1, d, tn), lambda l, j: (l, 0, j)),
                  pl.BlockSpec((1, 1, tn), lambda l, j: (l, 0, j))],
        out_specs=pl.BlockSpec((1, 8, tn), lambda l, j: (l, 0, j)),
        out_shape=jax.ShapeDtypeStruct((depth, 8, n), F32),
        compiler_params=_cparams(("parallel", "parallel")),
        name="adaln",
    )(cond, ada_w, ada_b.reshape(depth, 1, n))


def _modnorm_body(x_ref, g_ref, sc_ref, sh_ref, o_ref):
    x = x_ref[...]
    y = x * lax.rsqrt(jnp.mean(x * x, axis=-1, keepdims=True) + EPS)
    y = y * g_ref[...]
    o_ref[...] = (y * (1.0 + sc_ref[0]) + sh_ref[0]).astype(o_ref.dtype)


def modnorm(x, g, scale, shift, rows_per_group, out_dtype):
    r, d = x.shape
    tm = 256
    ngroups = scale.shape[0]
    tiles_per_group = rows_per_group // tm
    gmap = lambda i: (jnp.minimum(i // tiles_per_group, ngroups - 1), 0, 0)
    return pl.pallas_call(
        _modnorm_body,
        grid=(r // tm,),
        in_specs=[pl.BlockSpec((tm, d), lambda i: (i, 0)),
                  pl.BlockSpec((1, d), lambda i: (0, 0)),
                  pl.BlockSpec((1, 1, d), gmap),
                  pl.BlockSpec((1, 1, d), gmap)],
        out_specs=pl.BlockSpec((tm, d), lambda i: (i, 0)),
        out_shape=jax.ShapeDtypeStruct((r, d), out_dtype),
        compiler_params=_cparams(("parallel",)),
        name="modnorm",
    )(x, g.reshape(1, d), scale.reshape(ngroups, 1, d), shift.reshape(ngroups, 1, d))


def _mm_body(a_ref, b_ref, o_ref):
    o_ref[...] = jnp.dot(a_ref[...], b_ref[...], preferred_element_type=F32).astype(o_ref.dtype)


def matmul(a, b, tm, tn, out_dtype):
    m, k = a.shape
    n = b.shape[1]
    return pl.pallas_call(
        _mm_body,
        grid=(n // tn, m // tm),
        in_specs=[pl.BlockSpec((tm, k), lambda j, i: (i, 0)),
                  pl.BlockSpec((k, tn), lambda j, i: (0, j))],
        out_specs=pl.BlockSpec((tm, tn), lambda j, i: (i, j)),
        out_shape=jax.ShapeDtypeStruct((m, n), out_dtype),
        compiler_params=_cparams(("parallel", "parallel")),
        name="matmul",
    )(a, b)


def _mm_res_body(a1_ref, a2_ref, b_ref, x_ref, g_ref, o_ref):
    k1 = a1_ref.shape[1]
    acc = jnp.dot(a1_ref[...], b_ref[:k1, :], preferred_element_type=F32)
    acc = acc + jnp.dot(a2_ref[...], b_ref[k1:, :], preferred_element_type=F32)
    o_ref[...] = x_ref[...] + g_ref[0] * acc


def matmul_residual(a1, a2, b, x, gate, rows_per_group, tm, tn):
    m, k1 = a1.shape
    k2 = a2.shape[1]
    n = b.shape[1]
    ngroups = gate.shape[0]
    tiles_per_group = rows_per_group // tm
    return pl.pallas_call(
        _mm_res_body,
        grid=(n // tn, m // tm),
        in_specs=[pl.BlockSpec((tm, k1), lambda j, i: (i, 0)),
                  pl.BlockSpec((tm, k2), lambda j, i: (i, 0)),
                  pl.BlockSpec((k1 + k2, tn), lambda j, i: (0, j)),
                  pl.BlockSpec((tm, tn), lambda j, i: (i, j)),
                  pl.BlockSpec((1, 1, tn), lambda j, i: (jnp.minimum(i // tiles_per_group, ngroups - 1), 0, j))],
        out_specs=pl.BlockSpec((tm, tn), lambda j, i: (i, j)),
        out_shape=jax.ShapeDtypeStruct((m, n), F32),
        compiler_params=_cparams(("parallel", "parallel")),
        name="matmul_residual",
    )(a1, a2, b, x, gate.reshape(ngroups, 1, n))


def _rope_half(y, cos2, sin2):
    return y * cos2 + pltpu.roll(y, GQA_HEAD_DIM // 2, 1) * sin2


def _rope_quarter(v, cosb, sb1, sb2):
    return v * cosb + pltpu.roll(v, 96, 1) * sb1 + pltpu.roll(v, 32, 1) * sb2


def _rms(x, g):
    return x * lax.rsqrt(jnp.mean(x * x, axis=-1, keepdims=True) + EPS) * g


def _prep_a_body(qa_ref, ka_ref, cq_ref, ckv_ref, kr_ref, cosa_ref, sina_ref, cosb_ref, sb1_ref, sb2_ref,
                 qn_ref, kn_ref, cqg_ref, ckvg_ref, qg_ref, kg_ref, cqn_ref, ckvn_ref, krp_ref):
    cosa, sina = cosa_ref[...], sina_ref[...]
    hd = GQA_HEAD_DIM
    q_scale = hd ** -0.5 * LOG2E
    for h in range(GQA_HEADS):
        y = _rms(qa_ref[:, h * hd:(h + 1) * hd].astype(F32), qn_ref[...])
        qg_ref[:, h * hd:(h + 1) * hd] = (_rope_half(y, cosa, sina) * q_scale).astype(BF16)
    for h in range(GQA_KV_HEADS):
        y = _rms(ka_ref[:, h * hd:(h + 1) * hd].astype(F32), kn_ref[...])
        kg_ref[:, h * hd:(h + 1) * hd] = _rope_half(y, cosa, sina).astype(BF16)
    cqn_ref[...] = _rms(cq_ref[...].astype(F32), cqg_ref[...]).astype(BF16)
    ckvn_ref[...] = _rms(ckv_ref[...].astype(F32), ckvg_ref[...]).astype(BF16)
    krp_ref[...] = _rope_quarter(kr_ref[...].astype(F32), cosb_ref[...], sb1_ref[...], sb2_ref[...]).astype(BF16)


def attn_prep(p, tabs, tab_rows, qn_g, kn_g, cq_g, ckv_g):
    r = p.shape[0]
    tm = 256
    tab_tiles = tab_rows // tm
    rowmap = lambda c: (lambda i: (i, c))
    tabmap = lambda i: (i % tab_tiles, 0)
    vec = lambda n: pl.BlockSpec((1, n), lambda i: (0, 0))
    outs = pl.pallas_call(
        _prep_a_body,
        grid=(r // tm,),
        in_specs=[pl.BlockSpec((tm, 2048), rowmap(ATT_QA // 2048)),
                  pl.BlockSpec((tm, 512), rowmap(ATT_KA // 512)),
                  pl.BlockSpec((tm, 1024), rowmap(ATT_CQ // 1024)),
                  pl.BlockSpec((tm, 512), rowmap(ATT_CKV // 512)),
                  pl.BlockSpec((tm, 128), rowmap(ATT_KR // 128)),
                  ] + [pl.BlockSpec((tm, 128), tabmap)] * 5 + [vec(128), vec(128), vec(1024), vec(512)],
        out_specs=[pl.BlockSpec((tm, 2048), lambda i: (i, 0)),
                   pl.BlockSpec((tm, 512), lambda i: (i, 0)),
                   pl.BlockSpec((tm, 1024), lambda i: (i, 0)),
                   pl.BlockSpec((tm, 512), lambda i: (i, 0)),
                   pl.BlockSpec((tm, 128), lambda i: (i, 0))],
        out_shape=[jax.ShapeDtypeStruct((r, 2048), BF16), jax.ShapeDtypeStruct((r, 512), BF16),
                   jax.ShapeDtypeStruct((r, 1024), BF16), jax.ShapeDtypeStruct((r, 512), BF16),
                   jax.ShapeDtypeStruct((r, 128), BF16)],
        compiler_params=_cparams(("parallel",)),
        name="attn_prep",
    )(p, p, p, p, p, *tabs, qn_g.reshape(1, -1), kn_g.reshape(1, -1), cq_g.reshape(1, -1), ckv_g.reshape(1, -1))
    return outs


def _mlaq_body(a_ref, b_ref, cosb_ref, sb1_ref, sb2_ref, o_ref):
    acc = jnp.dot(a_ref[...], b_ref[...], preferred_element_type=F32)
    scale = (MLA_NOPE + MLA_ROPE) ** -0.5 * LOG2E
    cosb, sb1, sb2 = cosb_ref[...], sb1_ref[...], sb2_ref[...]
    for h in range(acc.shape[1] // MLA_QK_PAD):
        c0 = h * MLA_QK_PAD
        o_ref[:, c0:c0 + 128] = (acc[:, c0:c0 + 128] * scale).astype(BF16)
        o_ref[:, c0 + 128:c0 + 256] = (_rope_quarter(acc[:, c0 + 128:c0 + 256], cosb, sb1, sb2) * scale).astype(BF16)


def mla_q(cqn, w_uq_pad, tabs_b, tab_rows, tm):
    m, k = cqn.shape
    n = w_uq_pad.shape[1]
    tn = 1024
    tab_tiles = tab_rows // tm
    return pl.pallas_call(
        _mlaq_body,
        grid=(n // tn, m // tm),
        in_specs=[pl.BlockSpec((tm, k), lambda j, i: (i, 0)),
                  pl.BlockSpec((k, tn), lambda j, i: (0, j))] +
                 [pl.BlockSpec((tm, 128), lambda j, i: (i % tab_tiles, 0))] * 3,
        out_specs=pl.BlockSpec((tm, tn), lambda j, i: (i, j)),
        out_shape=jax.ShapeDtypeStruct((m, n), BF16),
        compiler_params=_cparams(("parallel", "parallel")),
        name="mla_q",
    )(cqn, w_uq_pad, *tabs_b)


def _flash_body(*refs, has_lat, kparts, group, tn, tk, s_lat):
    idx = 1
    q_ref = refs[0]
    if has_lat:
        klat = refs[idx:idx + kparts]
        vlat = refs[idx + kparts]
        idx += kparts + 1
    kctx = refs[idx:idx + kparts]
    vctx = refs[idx + kparts]
    idx += kparts + 1
    g_ref, o_ref, m_ref, l_ref, acc_ref, s_ref = refs[idx:idx + 6]

    dv = acc_ref.shape[0]
    ntiles = acc_ref.shape[1] // tn
    dq = q_ref.shape[1] // group
    n_ctx = vctx.shape[0]

    def q_tile(j):
        return q_ref[:, j * dq:(j + 1) * dq] if group > 1 else q_ref[j * tn:(j + 1) * tn, :]

    m_ref[...] = jnp.full(m_ref.shape, -1e30, F32)
    l_ref[...] = jnp.zeros(l_ref.shape, F32)
    acc_ref[...] = jnp.zeros(acc_ref.shape, F32)

    def cat(parts):
        return parts[0] if len(parts) == 1 else jnp.concatenate(parts, axis=1)

    def produce(slot, krefs, off, nk):
        rows = slice(None) if off is None else pl.ds(off, nk)
        kc = cat([r[rows, :] for r in krefs])
        for j in range(ntiles):
            s_ref[slot, :nk, j * tn:(j + 1) * tn] = lax.dot_general(
                kc, q_tile(j), (((1,), (1,)), ((), ())), preferred_element_type=F32)

    def consume(slot, vref, off, nk):
        vc = vref[...] if off is None else vref[pl.ds(off, nk), :]
        for j in range(ntiles):
            sl = slice(j * tn, (j + 1) * tn)
            s = s_ref[slot, :nk, sl]
            m_prev = m_ref[:, sl]
            m_new = jnp.maximum(m_prev, jnp.max(s, axis=0, keepdims=True))
            alpha = jnp.exp2(m_prev - m_new)
            p = jnp.exp2(s - m_new)
            l_ref[:, sl] = alpha * l_ref[:, sl] + jnp.sum(p, axis=0, keepdims=True)
            pv = lax.dot_general(vc, p.astype(BF16), (((0,), (0,)), ((), ())), preferred_element_type=F32)
            acc_ref[:, sl] = alpha * acc_ref[:, sl] + pv
            m_ref[:, sl] = m_new

    if has_lat:
        nchunks = s_lat // tk
        assert nchunks % 2 == 0 and nchunks >= 2
        produce(0, klat, 0, tk)

        def pair(i, carry):
            c0 = pl.multiple_of(2 * i * tk, tk)
            c1 = pl.multiple_of(c0 + tk, tk)
            c2 = pl.multiple_of(c0 + 2 * tk, tk)
            produce(1, klat, c1, tk)
            consume(0, vlat, c0, tk)
            produce(0, klat, c2, tk)
            consume(1, vlat, c1, tk)
            return carry

        lax.fori_loop(0, nchunks // 2 - 1, pair, 0)
        last = (nchunks - 2) * tk
        produce(1, klat, last + tk, tk)
        consume(0, vlat, last, tk)
        produce(0, kctx, None, n_ctx)
        consume(1, vlat, last + tk, tk)
    else:
        produce(0, kctx, None, n_ctx)
    consume(0, vctx, None, n_ctx)

    for j in range(ntiles):
        sl = slice(j * tn, (j + 1) * tn)
        o = jnp.transpose(acc_ref[:, sl] / l_ref[:, sl])
        if group > 1:
            gt = g_ref[:, j * dv:(j + 1) * dv].astype(F32)
            o_ref[:, j * dv:(j + 1) * dv] = (o * _silu(gt)).astype(BF16)
        else:
            gt = g_ref[sl, :].astype(F32)
            o_ref[sl, :] = (o * _silu(gt)).astype(BF16)


def flash(q, klat, vlat, kctx, vctx, gate, *, batch, heads, group, dqk, s_lat, s_ctx, tq, q_is_ctx):
    dv = 128
    kparts = len(kctx)
    has_lat = not q_is_ctx
    nq = (s_ctx if q_is_ctx else s_lat) // tq
    tk = FLASH_TK
    qmap = lambda b, h, i: (b * nq + i, h)
    in_specs = [pl.BlockSpec((tq, group * dqk), qmap)]
    args = [q]
    if has_lat:
        for arr, c0 in klat:
            in_specs.append(pl.BlockSpec((s_lat, 128), functools.partial(lambda b, h, i, c0, per: (b, c0 + per * h), c0=c0[0], per=c0[1])))
            args.append(arr)
        in_specs.append(pl.BlockSpec((s_lat, dv), lambda b, h, i: (b, vlat[1] + h)))
        args.append(vlat[0])
    for arr, c0 in kctx:
        in_specs.append(pl.BlockSpec((s_ctx, 128), functools.partial(lambda b, h, i, c0, per: (b, c0 + per * h), c0=c0[0], per=c0[1])))
        args.append(arr)
    in_specs.append(pl.BlockSpec((s_ctx, dv), lambda b, h, i: (b, vctx[1] + h)))
    args.append(vctx[0])
    in_specs.append(pl.BlockSpec((tq, group * dv), lambda b, h, i: (b * nq + i, gate[1] + h)))
    args.append(gate[0])
    rows = q.shape[0]
    tn = min(FLASH_TN, tq)
    assert tq % tn == 0 and (group == 1 or tq == tn)
    assert s_ctx <= tk
    body = functools.partial(_flash_body, has_lat=has_lat, kparts=kparts, group=group, tn=tn, tk=tk, s_lat=s_lat)
    return pl.pallas_call(
        body,
        grid=(batch, heads, nq),
        in_specs=in_specs,
        out_specs=pl.BlockSpec((tq, group * dv), qmap),
        out_shape=jax.ShapeDtypeStruct((rows, heads * group * dv), BF16),
        scratch_shapes=[pltpu.VMEM((1, group * tq), F32), pltpu.VMEM((1, group * tq), F32),
                        pltpu.VMEM((dv, group * tq), F32), pltpu.VMEM((2, tk, group * tq), F32)],
        compiler_params=_cparams(("parallel", "parallel", "parallel")),
        name="flash_ctx" if q_is_ctx else "flash_lat",
    )(*args)


def _gla_body(*refs, rev, with_out, has_init, nsteps):
    c = GLA_CHUNK
    idx = 0
    if with_out:
        q_ref = refs[0]
        idx = 1
    k_ref, v_ref, gd_ref, wg_ref, bg_ref = refs[idx:idx + 5]
    idx += 5
    if has_init:
        s0_ref = refs[idx]
        idx += 1
    out_ref = refs[idx]
    st_ref = refs[idx + 1]
    oacc_ref = refs[idx + 2] if with_out else None

    step = pl.program_id(2)

    @pl.when(step == 0)
    def _():
        if has_init:
            st_ref[...] = s0_ref[0, 0]
        else:
            st_ref[...] = jnp.zeros(st_ref.shape, F32)

    v = v_ref[...]
    kf = k_ref[...].astype(F32)
    z = jnp.dot(gd_ref[...].astype(BF16), wg_ref[0], preferred_element_type=F32) + bg_ref[0]
    lg = (jnp.minimum(z, 0.0) - jnp.log(1.0 + jnp.exp(-jnp.abs(z)))) * (1.0 / GLA_GATE_TAU)
    row = lax.broadcasted_iota(jnp.int32, (c, c), 0)
    col = lax.broadcasted_iota(jnp.int32, (c, c), 1)
    tri = jnp.where((row <= col) if rev else (row >= col), 1.0, 0.0).astype(F32)
    cum = jnp.dot(tri, lg, preferred_element_type=F32, precision=lax.Precision.HIGHEST)
    tot = cum[0:1] if rev else cum[c - 1:c]
    st = st_ref[...]

    if with_out:
        qf = q_ref[...].astype(F32) * (GLA_DK ** -0.5)
        qe = (qf * jnp.exp(cum)).astype(BF16)
        oacc_ref[...] = lax.dot_general(qe, st.astype(BF16), (((1,), (1,)), ((), ())), preferred_element_type=F32)
        blk = c // 2
        while blk >= GLA_SUB:
            for a in range(0, c, 2 * blk):
                if rev:
                    i0, j0, cref = a, a + blk, cum[a + blk:a + blk + 1]
                else:
                    i0, j0, cref = a + blk, a, cum[a + blk - 1:a + blk]
                qi = (qf[i0:i0 + blk] * jnp.exp(cum[i0:i0 + blk] - cref)).astype(BF16)
                kj = (kf[j0:j0 + blk] * jnp.exp(cref - cum[j0:j0 + blk])).astype(BF16)
                att = lax.dot_general(qi, kj, (((1,), (1,)), ((), ())), preferred_element_type=F32)
                oacc_ref[i0:i0 + blk, :] += jnp.dot(att.astype(BF16), v[j0:j0 + blk], preferred_element_type=F32)
            blk //= 2
        sub = GLA_SUB
        ii = lax.broadcasted_iota(jnp.int32, (sub, sub), 0)
        jj = lax.broadcasted_iota(jnp.int32, (sub, sub), 1)
        valid = (jj >= ii) if rev else (jj <= ii)
        for a in range(0, c, sub):
            qs, ks, cs = qf[a:a + sub], kf[a:a + sub], cum[a:a + sub]
            att = jnp.zeros((sub, sub), F32)
            for j in range(sub):
                e = jnp.exp(jnp.minimum(cs - cs[j:j + 1], 0.0))
                colv = jnp.sum(qs * e * ks[j:j + 1], axis=-1, keepdims=True)
                att = att + colv * jnp.where(jj[0:1] == j, 1.0, 0.0)
            att = jnp.where(valid, att, 0.0)
            oacc_ref[a:a + sub, :] += jnp.dot(att.astype(BF16), v[a:a + sub], preferred_element_type=F32)
        out_ref[...] = oacc_ref[...].astype(out_ref.dtype)

    kt = (kf * jnp.exp(tot - cum)).astype(BF16)
    st_new = st * jnp.exp(tot) + lax.dot_general(v, kt, (((0,), (0,)), ((), ())), preferred_element_type=F32)
    st_ref[...] = st_new

    if not with_out:
        @pl.when(step == nsteps - 1)
        def _():
            out_ref[0, 0] = st_new


def gla(q, k, v, gd, wg_pad, bg, s0, *, batch, rows_per_batch, rev):
    c = GLA_CHUNK
    nsteps = rows_per_batch // c
    with_out = q is not None
    has_init = s0 is not None

    def rmap(off):
        if rev:
            return lambda b, h, s: (b * nsteps + nsteps - 1 - s, off + h)
        return lambda b, h, s: (b * nsteps + s, off + h)

    if rev:
        gdmap = lambda b, h, s: (b * nsteps + nsteps - 1 - s, gd[1])
    else:
        gdmap = lambda b, h, s: (b * nsteps + s, gd[1])
    in_specs, args = [], []
    if with_out:
        in_specs.append(pl.BlockSpec((c, GLA_DK), rmap(q[1])))
        args.append(q[0])
    in_specs += [pl.BlockSpec((c, GLA_DK), rmap(k[1])),
                 pl.BlockSpec((c, GLA_DV), rmap(v[1])),
                 pl.BlockSpec((c, 128), gdmap),
                 pl.BlockSpec((1, 128, GLA_DK), lambda b, h, s: (h, 0, 0)),
                 pl.BlockSpec((1, 1, GLA_DK), lambda b, h, s: (h, 0, 0))]
    args += [k[0], v[0], gd[0], wg_pad, bg]
    if has_init:
        in_specs.append(pl.BlockSpec((1, 1, GLA_DV, GLA_DK), lambda b, h, s: (b, h, 0, 0)))
        args.append(s0)
    scratch = [pltpu.VMEM((GLA_DV, GLA_DK), F32)]
    if with_out:
        out_specs = pl.BlockSpec((c, GLA_DV), rmap(0))
        out_shape = jax.ShapeDtypeStruct((batch * rows_per_batch, GLA_HEADS * GLA_DV), BF16)
        scratch.append(pltpu.VMEM((c, GLA_DV), F32))
    else:
        out_specs = pl.BlockSpec((1, 1, GLA_DV, GLA_DK), lambda b, h, s: (b, h, 0, 0))
        out_shape = jax.ShapeDtypeStruct((batch, GLA_HEADS, GLA_DV, GLA_DK), F32)
    body = functools.partial(_gla_body, rev=rev, with_out=with_out, has_init=has_init, nsteps=nsteps)
    return pl.pallas_call(
        body,
        grid=(batch, GLA_HEADS, nsteps),
        in_specs=in_specs,
        out_specs=out_specs,
        out_shape=out_shape,
        scratch_shapes=scratch,
        compiler_params=_cparams(("parallel", "parallel", "arbitrary")),
        name="gla_scan" if with_out else "gla_state",
    )(*args)


def _fft_a_body(x_ref, gc_ref, gs_ref, tr_ref, ti_ref, o_ref):
    n1, t2, ch = x_ref.shape
    x = x_ref[...].reshape(n1 * t2, ch)
    yr = jnp.dot(gc_ref[...], x, preferred_element_type=F32)
    yi = -jnp.dot(gs_ref[...], x, preferred_element_type=F32)
    tr, ti = tr_ref[...], ti_ref[...]
    o_ref[:, :, :ch] = (yr * tr - yi * ti).astype(BF16).reshape(n1, t2, ch)
    o_ref[:, :, ch:] = (yr * ti + yi * tr).astype(BF16).reshape(n1, t2, ch)


def _fft_b_body(y_ref, fc_ref, fs_ref, cc_ref, sc_ref, o_ref, *, norm):
    ch = y_ref.shape[1] // 2
    yr, yi = y_ref[:, :ch], y_ref[:, ch:]
    fc, fs = fc_ref[...], fs_ref[...]
    zr = jnp.dot(fc, yr, preferred_element_type=F32) + jnp.dot(fs, yi, preferred_element_type=F32)
    zi = jnp.dot(fc, yi, preferred_element_type=F32) - jnp.dot(fs, yr, preferred_element_type=F32)
    for g in range(FNET_GROUPS):
        sl = slice(g * FNET_DIM, (g + 1) * FNET_DIM)
        f = jnp.dot(zr[:, sl].astype(BF16), cc_ref[...], preferred_element_type=F32)
        f = f + jnp.dot(zi[:, sl].astype(BF16), sc_ref[...], preferred_element_type=F32)
        o_ref[:, sl] = (f * norm).astype(BF16)


def _dft_tables(s):
    n1, n2, t2 = FFT_N1, s // FFT_N1, FFT_T2
    a = np.arange(n1)
    ang1 = 2.0 * np.pi * ((a[:, None] * a[None, :]) % n1) / n1
    eye = np.eye(t2)
    gc = np.kron(np.cos(ang1), eye)
    gs = np.kron(np.sin(ang1), eye)
    b = np.arange(n2)
    angt = 2.0 * np.pi * ((a[:, None] * b[None, :]) % s) / s
    tw = angt.reshape(n1, n2 // t2, t2).transpose(1, 0, 2).reshape(n2 // t2, n1 * t2, 1)
    ang2 = 2.0 * np.pi * ((b[:, None] * b[None, :]) % n2) / n2
    d = np.arange(FNET_DIM)
    angc = 2.0 * np.pi * ((d[:, None] * d[None, :]) % FNET_DIM) / FNET_DIM
    bf = lambda m: jnp.asarray(m, dtype=BF16)
    return dict(gc=bf(gc), gs=bf(gs), tr=jnp.asarray(np.cos(tw), F32), ti=jnp.asarray(-np.sin(tw), F32),
                fc=bf(np.cos(ang2)), fs=bf(np.sin(ang2)), cc=bf(np.cos(angc)), sc=bf(np.sin(angc)))


def fourier_mix(p, u_col, batch, s):
    n1, n2, t2 = FFT_N1, s // FFT_N1, FFT_T2
    ch = FNET_GROUPS * FNET_DIM
    tabs = _dft_tables(s)
    p4 = p.reshape(batch, n1, n2, p.shape[1])
    full = lambda shape: pl.BlockSpec(shape, lambda b, j: (0,) * len(shape))
    y = pl.pallas_call(
        _fft_a_body,
        grid=(batch, n2 // t2),
        in_specs=[pl.BlockSpec((None, n1, t2, ch), lambda b, j: (b, 0, j, u_col // ch)),
                  full((n1 * t2, n1 * t2)), full((n1 * t2, n1 * t2)),
                  pl.BlockSpec((None, n1 * t2, 1), lambda b, j: (j, 0, 0)),
                  pl.BlockSpec((None, n1 * t2, 1), lambda b, j: (j, 0, 0))],
        out_specs=pl.BlockSpec((None, n1, t2, 2 * ch), lambda b, j: (b, 0, j, 0)),
        out_shape=jax.ShapeDtypeStruct((batch, n1, n2, 2 * ch), BF16),
        compiler_params=_cparams(("parallel", "parallel")),
        name="fft_stage_a",
    )(p4, tabs["gc"], tabs["gs"], tabs["tr"], tabs["ti"])
    norm = float(1.0 / np.sqrt(s * FNET_DIM))
    f = pl.pallas_call(
        functools.partial(_fft_b_body, norm=norm),
        grid=(batch, n1),
        in_specs=[pl.BlockSpec((None, None, n2, 2 * ch), lambda b, j: (b, j, 0, 0)),
                  full((n2, n2)), full((n2, n2)), full((FNET_DIM, FNET_DIM)), full((FNET_DIM, FNET_DIM))],
        out_specs=pl.BlockSpec((None, None, n2, ch), lambda b, j: (b, j, 0, 0)),
        out_shape=jax.ShapeDtypeStruct((batch, n1, n2, ch), BF16),
        compiler_params=_cparams(("parallel", "parallel")),
        name="fft_stage_b",
    )(y, tabs["fc"], tabs["fs"], tabs["cc"], tabs["sc"])
    return f.transpose(0, 2, 1, 3).reshape(batch * s, ch)


def _finish_body(of_ref, ob_ref, f_ref, g_ref, ong_ref, o_ref):
    dv = GLA_DV
    for h in range(GLA_HEADS):
        sl = slice(h * dv, (h + 1) * dv)
        o = of_ref[:, sl].astype(F32) + ob_ref[:, sl].astype(F32)
        y = _rms(o, ong_ref[...])
        gt = g_ref[:, sl].astype(F32)
        o_ref[:, sl] = (y * _silu(gt)).astype(BF16)
    base = GLA_HEADS * dv
    gt = g_ref[:, base:].astype(F32)
    o_ref[:, base:] = (f_ref[...].astype(F32) * _silu(gt)).astype(BF16)


def rec_finish(o_f, o_b, f, p, on_g):
    r = o_f.shape[0]
    tm = 256
    w = GLA_HEADS * GLA_DV
    mix = w + FNET_GROUPS * FNET_DIM
    return pl.pallas_call(
        _finish_body,
        grid=(r // tm,),
        in_specs=[pl.BlockSpec((tm, w), lambda i: (i, 0)),
                  pl.BlockSpec((tm, w), lambda i: (i, 0)),
                  pl.BlockSpec((tm, mix - w), lambda i: (i, 0)),
                  pl.BlockSpec((tm, mix), lambda i: (i, REC_GATE // mix)),
                  pl.BlockSpec((1, GLA_DV), lambda i: (0, 0))],
        out_specs=pl.BlockSpec((tm, mix), lambda i: (i, 0)),
        out_shape=jax.ShapeDtypeStruct((r, mix), BF16),
        compiler_params=_cparams(("parallel",)),
        name="rec_finish",
    )(o_f, o_b, f, p, on_g.reshape(1, -1))


def _rope_tables(s):
    t = np.arange(s)
    row, col = (t // GRID_W).astype(np.float32), (t % GRID_W).astype(np.float32)

    def cs(rot_dim):
        quarter = rot_dim // 4
        inv_freq = (np.float32(ROPE_THETA) ** (-np.arange(quarter, dtype=np.float32) / quarter)).astype(np.float32)
        ang = np.concatenate([row[:, None] * inv_freq, col[:, None] * inv_freq], axis=-1).astype(np.float32)
        return np.cos(ang), np.sin(ang)

    ca, sa = cs(GQA_HEAD_DIM)
    cb, sb = cs(MLA_ROPE)
    z32, z64 = np.zeros((s, 32), np.float32), np.zeros((s, 64), np.float32)
    tabs = [np.concatenate([ca, ca], 1), np.concatenate([-sa, sa], 1),
            np.concatenate([cb, cb, z64], 1), np.concatenate([-sb, z32, z64], 1), np.concatenate([z32, sb, z64], 1)]
    return [jnp.asarray(t_, F32) for t_ in tabs]


def _identity_rope_tables(n):
    one, zero = np.ones((n, 128), np.float32), np.zeros((n, 128), np.float32)
    cb = np.concatenate([np.ones((n, 64), np.float32), np.zeros((n, 64), np.float32)], 1)
    return [jnp.asarray(t_, F32) for t_ in (one, zero, cb, zero, zero)]


def attention_calls(p_lat, p_ctx, qg_l, qg_c, kg_l, kg_c, qm_l, qm_c, kv_l, kv_c, krp_l, krp_c, batch, s, n_ctx):
    va_col, gate_a_col, gate_b_col = ATT_VA // 128, ATT_GATE // 512, (ATT_GATE + 2048) // 128
    common = dict(batch=batch, s_lat=s, s_ctx=n_ctx)
    gqa = dict(heads=GQA_KV_HEADS, group=GQA_GROUP, dqk=GQA_HEAD_DIM, **common)
    mla = dict(heads=MLA_HEADS, group=1, dqk=MLA_QK_PAD, **common)
    kb_l, kb_c = [(kv_l, (0, 1)), (krp_l, (0, 0))], [(kv_c, (0, 1)), (krp_c, (0, 0))]
    ya_l = flash(qg_l, [(kg_l, (0, 1))], (p_lat, va_col), [(kg_c, (0, 1))], (p_ctx, va_col), (p_lat, gate_a_col),
                 tq=FLASH_TN, q_is_ctx=False, **gqa)
    yb_l = flash(qm_l, kb_l, (kv_l, MLA_HEADS), kb_c, (kv_c, MLA_HEADS), (p_lat, gate_b_col),
                 tq=2 * FLASH_TN, q_is_ctx=False, **mla)
    ya_c = flash(qg_c, None, None, [(kg_c, (0, 1))], (p_ctx, va_col), (p_ctx, gate_a_col),
                 tq=n_ctx, q_is_ctx=True, **gqa)
    yb_c = flash(qm_c, None, None, kb_c, (kv_c, MLA_HEADS), (p_ctx, gate_b_col), tq=n_ctx, q_is_ctx=True, **mla)
    return ya_l, ya_c, yb_l, yb_c


def _attention_layer(x_lat, x_ctx, mod, norm_g, w_in, qn_g, kn_g, cq_g, ckv_g, w_uq, w_ukv, w_out, batch, s, n_ctx):
    d = x_lat.shape[1]
    shift, scale, gate = mod[:, :d], mod[:, d:2 * d], mod[:, 2 * d:]
    h_lat = modnorm(x_lat, norm_g, scale[:batch], shift[:batch], s, BF16)
    h_ctx = modnorm(x_ctx, norm_g, scale[batch:batch + 1], shift[batch:batch + 1], batch * n_ctx, BF16)

    w_in_p = jnp.concatenate([w_in[:, :4672], jnp.zeros((d, ATT_GATE - 4672), w_in.dtype), w_in[:, 4672:]], axis=1).astype(BF16)
    w_uq_p = jnp.pad(w_uq.reshape(MLA_Q_LORA, MLA_HEADS, MLA_NOPE + MLA_ROPE),
                     ((0, 0), (0, 0), (0, MLA_QK_PAD - MLA_NOPE - MLA_ROPE))).reshape(MLA_Q_LORA, MLA_HEADS * MLA_QK_PAD).astype(BF16)
    w_ukv3 = w_ukv.reshape(MLA_KV_LORA, MLA_HEADS, MLA_NOPE + MLA_V)
    w_ukv_p = jnp.concatenate([w_ukv3[:, :, :MLA_NOPE].reshape(MLA_KV_LORA, -1),
                               w_ukv3[:, :, MLA_NOPE:].reshape(MLA_KV_LORA, -1)], axis=1).astype(BF16)
    w_out_b = w_out.astype(BF16)

    ctx_rows = batch * n_ctx
    p_lat = matmul(h_lat, w_in_p, 1024, 1024, BF16)
    p_ctx = matmul(h_ctx, w_in_p, ctx_rows, 1024, BF16)

    tabs_lat, tabs_ctx = _rope_tables(s), _identity_rope_tables(ctx_rows)
    qg_l, kg_l, cqn_l, ckvn_l, krp_l = attn_prep(p_lat, tabs_lat, s, qn_g, kn_g, cq_g, ckv_g)
    qg_c, kg_c, cqn_c, ckvn_c, krp_c = attn_prep(p_ctx, tabs_ctx, ctx_rows, qn_g, kn_g, cq_g, ckv_g)
    qm_l = mla_q(cqn_l, w_uq_p, tabs_lat[2:], s, 1024)
    qm_c = mla_q(cqn_c, w_uq_p, tabs_ctx[2:], ctx_rows, ctx_rows)
    kv_l = matmul(ckvn_l, w_ukv_p, 1024, 1024, BF16)
    kv_c = matmul(ckvn_c, w_ukv_p, ctx_rows, 1024, BF16)

    ya_l, ya_c, yb_l, yb_c = attention_calls(p_lat, p_ctx, qg_l, qg_c, kg_l, kg_c, qm_l, qm_c, kv_l, kv_c, krp_l, krp_c,
                                             batch, s, n_ctx)

    x_lat = matmul_residual(ya_l, yb_l, w_out_b, x_lat, gate[:batch], s, 1024, 512)
    x_ctx = matmul_residual(ya_c, yb_c, w_out_b, x_ctx, gate[batch:batch + 1], ctx_rows, ctx_rows, 512)
    return x_lat, x_ctx


def _recurrent_layer(x_lat, x_ctx, mod, norm_g, w_in, wg_f, bg_f, wg_b, bg_b, on_g, w_out, batch, s, n_ctx):
    d = x_lat.shape[1]
    shift, scale, gate = mod[:, :d], mod[:, d:2 * d], mod[:, 2 * d:]
    h_lat = modnorm(x_lat, norm_g, scale[:batch], shift[:batch], s, BF16)
    h_ctx = modnorm(x_ctx, norm_g, scale[batch:batch + 1], shift[batch:batch + 1], batch * n_ctx, BF16)

    w_main = jnp.concatenate([w_in[:, 7200:], w_in[:, :6144], w_in[:, 6176:7200]], axis=1).astype(BF16)
    w_gd = jnp.pad(w_in[:, 6144:6176], ((0, 0), (0, 96))).astype(BF16)
    w_out_b = w_out.astype(BF16)

    def gate_w(wg, row0):
        w3 = wg.reshape(GLA_GATE_RANK, GLA_HEADS, GLA_DK).transpose(1, 0, 2)
        return jnp.pad(w3, ((0, 0), (row0, 128 - GLA_GATE_RANK - row0), (0, 0))).astype(BF16)

    wgf_p, wgb_p = gate_w(wg_f, 0), gate_w(wg_b, GLA_GATE_RANK)
    bgf, bgb = bg_f.reshape(GLA_HEADS, 1, GLA_DK), bg_b.reshape(GLA_HEADS, 1, GLA_DK)

    ctx_rows = batch * n_ctx
    p_lat = matmul(h_lat, w_main, 1024, 1024, BF16)
    p_ctx = matmul(h_ctx, w_main, ctx_rows, 1024, BF16)
    gd_lat = matmul(h_lat, w_gd, 1024, 128, F32)
    gd_ctx = matmul(h_ctx, w_gd, ctx_rows, 128, F32)

    kq, kk, kv = REC_Q // GLA_DK, REC_K // GLA_DK, REC_V // GLA_DV
    sf = gla(None, (p_ctx, kk), (p_ctx, kv), (gd_ctx, 0), wgf_p, bgf, None, batch=batch, rows_per_batch=n_ctx, rev=False)
    sb = gla(None, (p_ctx, kk), (p_ctx, kv), (gd_ctx, 0), wgb_p, bgb, None, batch=batch, rows_per_batch=n_ctx, rev=True)
    o_f = gla((p_lat, kq), (p_lat, kk), (p_lat, kv), (gd_lat, 0), wgf_p, bgf, sf, batch=batch, rows_per_batch=s, rev=False)
    o_b = gla((p_lat, kq), (p_lat, kk), (p_lat, kv), (gd_lat, 0), wgb_p, bgb, sb, batch=batch, rows_per_batch=s, rev=True)

    f = fourier_mix(p_lat, REC_U, batch, s)
    y = rec_finish(o_f, o_b, f, p_lat, on_g)
    return matmul_residual_single(y, w_out_b, x_lat, gate[:batch], s, 1024, 512)


def _mm_res1_body(a_ref, b_ref, x_ref, g_ref, o_ref):
    acc = jnp.dot(a_ref[...], b_ref[...], preferred_element_type=F32)
    o_ref[...] = x_ref[...] + g_ref[0] * acc


def matmul_residual_single(a, b, x, gate, rows_per_group, tm, tn):
    m, k = a.shape
    n = b.shape[1]
    ngroups = gate.shape[0]
    tiles_per_group = rows_per_group // tm
    return pl.pallas_call(
        _mm_res1_body,
        grid=(n // tn, m // tm),
        in_specs=[pl.BlockSpec((tm, k), lambda j, i: (i, 0)),
                  pl.BlockSpec((k, tn), lambda j, i: (0, j)),
                  pl.BlockSpec((tm, tn), lambda j, i: (i, j)),
                  pl.BlockSpec((1, 1, tn), lambda j, i: (jnp.minimum(i // tiles_per_group, ngroups - 1), 0, j))],
        out_specs=pl.BlockSpec((tm, tn), lambda j, i: (i, j)),
        out_shape=jax.ShapeDtypeStruct((m, n), F32),
        compiler_params=_cparams(("parallel", "parallel")),
        name="matmul_residual1",
    )(a, b, x, gate.reshape(ngroups, 1, n))


def kernel(x, c, ctx, c_ctx, norm_g, ada_w, ada_b, att_w_in, att_qn_g, att_kn_g, mla_cq_g, mla_ckv_g, mla_w_uq,
           mla_w_ukv, att_w_out, rec_w_in, gla_wg_f, gla_bg_f, gla_wg_b, gla_bg_b, gla_on_g, rec_w_out, final_g):
    batch, s, d = x.shape
    n_ctx = ctx.shape[1]
    x_lat = x.reshape(batch * s, d)
    x_ctx = ctx.reshape(batch * n_ctx, d)
    cond = jnp.concatenate([c, c_ctx[None, :], jnp.zeros((8 - batch - 1, d), F32)], axis=0)
    mods = adaln(cond, ada_w, ada_b)

    x_lat, x_ctx = _attention_layer(x_lat, x_ctx, mods[0], norm_g[0], att_w_in[0], att_qn_g[0], att_kn_g[0],
                                    mla_cq_g[0], mla_ckv_g[0], mla_w_uq[0], mla_w_ukv[0], att_w_out[0],
                                    batch, s, n_ctx)
    x_lat = _recurrent_layer(x_lat, x_ctx, mods[1], norm_g[1], rec_w_in[0], gla_wg_f[0], gla_bg_f[0], gla_wg_b[0],
                             gla_bg_b[0], gla_on_g[0], rec_w_out[0], batch, s, n_ctx)
    zero = jnp.zeros((1, d), F32)
    out = modnorm(x_lat, final_g, zero, zero, batch * s, F32)
    return out.reshape(batch, s, d)
```

```python
import functools
import itertools

import numpy as np
import jax
import jax.numpy as jnp
from jax import lax
from jax.experimental import pallas as pl
from jax.experimental.pallas import tpu as pltpu

F32 = jnp.float32
BF16 = jnp.bfloat16

EPS = 1e-6
GRID_W = 64
ROPE_THETA = 10000.0

GQA_HEADS = 16
GQA_KV_HEADS = 4
GQA_GROUP = GQA_HEADS // GQA_KV_HEADS
GQA_HEAD_DIM = 128
MLA_HEADS = 16
MLA_Q_LORA = 1024
MLA_KV_LORA = 512
MLA_NOPE = 128
MLA_ROPE = 64
MLA_V = 128
MLA_QK_PAD = 256

GLA_HEADS = 6
GLA_DK = 256
GLA_DV = 512
GLA_GATE_RANK = 16
GLA_GATE_TAU = 16.0
GLA_CHUNK = 128
GLA_SUB = 8
FNET_GROUPS = 4
FNET_DIM = 256
FLASH_TN = 512
FLASH_TK = 1024
LOG2E = 1.4426950408889634
FFT_N1 = 64
FFT_T2 = 16

ATT_QA, ATT_KA, ATT_VA, ATT_CQ, ATT_CKV, ATT_KR, ATT_GATE, ATT_IN_PAD = 0, 2048, 2560, 3072, 4096, 4608, 5120, 9216
REC_GATE, REC_Q, REC_K, REC_V, REC_U, REC_IN_MAIN = 0, 4096, 5632, 7168, 10240, 11264

VMEM_LIMIT = 56 * 1024 * 1024


def _cparams(sem):
    return pltpu.CompilerParams(dimension_semantics=sem, vmem_limit_bytes=VMEM_LIMIT)


def _silu(x):
    return x * jax.nn.sigmoid(x)


def _adaln_body(a_ref, w_ref, b_ref, o_ref):
    a = _silu(a_ref[...]).astype(BF16)
    o_ref[0] = jnp.dot(a, w_ref[0].astype(BF16), preferred_element_type=F32) + b_ref[0]


def adaln(cond, ada_w, ada_b):
    depth, d, n = ada_w.shape
    tn = 512
    return pl.pallas_call(
        _adaln_body,
        grid=(depth, n // tn),
        in_specs=[pl.BlockSpec((8, d), lambda l, j: (0, 0)),
                  pl.BlockSpec((1, d, tn), lambda l, j: (l, 0, j)),
                  pl.BlockSpec((1, 1, tn), lambda l, j: (l, 0, j))],
        out_specs=pl.BlockSpec((1, 8, tn), lambda l, j: (l, 0, j)),
        out_shape=jax.ShapeDtypeStruct((depth, 8, n), F32),
        compiler_params=_cparams(("parallel", "parallel")),
        name="adaln",
    )(cond, ada_w, ada_b.reshape(depth, 1, n))


def _modnorm_body(x_ref, g_ref, sc_ref, sh_ref, o_ref):
    x = x_ref[...]
    y = x * lax.rsqrt(jnp.mean(x * x, axis=-1, keepdims=True) + EPS)
    y = y * g_ref[...]
    o_ref[...] = (y * (1.0 + sc_ref[0]) + sh_ref[0]).astype(o_ref.dtype)


def modnorm(x, g, scale, shift, rows_per_group, out_dtype):
    r, d = x.shape
    tm = 256
    ngroups = scale.shape[0]
    tiles_per_group = rows_per_group // tm
    gmap = lambda i: (jnp.minimum(i // tiles_per_group, ngroups - 1), 0, 0)
    return pl.pallas_call(
        _modnorm_body,
        grid=(r // tm,),
        in_specs=[pl.BlockSpec((tm, d), lambda i: (i, 0)),
                  pl.BlockSpec((1, d), lambda i: (0, 0)),
                  pl.BlockSpec((1, 1, d), gmap),
                  pl.BlockSpec((1, 1, d), gmap)],
        out_specs=pl.BlockSpec((tm, d), lambda i: (i, 0)),
        out_shape=jax.ShapeDtypeStruct((r, d), out_dtype),
        compiler_params=_cparams(("parallel",)),
        name="modnorm",
    )(x, g.reshape(1, d), scale.reshape(ngroups, 1, d), shift.reshape(ngroups, 1, d))


def _mm_body(a_ref, b_ref, o_ref):
    o_ref[...] = jnp.dot(a_ref[...], b_ref[...], preferred_element_type=F32).astype(o_ref.dtype)


def matmul(a, b, tm, tn, out_dtype):
    m, k = a.shape
    n = b.shape[1]
    return pl.pallas_call(
        _mm_body,
        grid=(n // tn, m // tm),
        in_specs=[pl.BlockSpec((tm, k), lambda j, i: (i, 0)),
                  pl.BlockSpec((k, tn), lambda j, i: (0, j))],
        out_specs=pl.BlockSpec((tm, tn), lambda j, i: (i, j)),
        out_shape=jax.ShapeDtypeStruct((m, n), out_dtype),
        compiler_params=_cparams(("parallel", "parallel")),
        name="matmul",
    )(a, b)


def _mm_res_body(a1_ref, a2_ref, b_ref, x_ref, g_ref, o_ref):
    k1 = a1_ref.shape[1]
    acc = jnp.dot(a1_ref[...], b_ref[:k1, :], preferred_element_type=F32)
    acc = acc + jnp.dot(a2_ref[...], b_ref[k1:, :], preferred_element_type=F32)
    o_ref[...] = x_ref[...] + g_ref[0] * acc


def matmul_residual(a1, a2, b, x, gate, rows_per_group, tm, tn):
    m, k1 = a1.shape
    k2 = a2.shape[1]
    n = b.shape[1]
    ngroups = gate.shape[0]
    tiles_per_group = rows_per_group // tm
    return pl.pallas_call(
        _mm_res_body,
        grid=(n // tn, m // tm),
        in_specs=[pl.BlockSpec((tm, k1), lambda j, i: (i, 0)),
                  pl.BlockSpec((tm, k2), lambda j, i: (i, 0)),
                  pl.BlockSpec((k1 + k2, tn), lambda j, i: (0, j)),
                  pl.BlockSpec((tm, tn), lambda j, i: (i, j)),
                  pl.BlockSpec((1, 1, tn), lambda j, i: (jnp.minimum(i // tiles_per_group, ngroups - 1), 0, j))],
        out_specs=pl.BlockSpec((tm, tn), lambda j, i: (i, j)),
        out_shape=jax.ShapeDtypeStruct((m, n), F32),
        compiler_params=_cparams(("parallel", "parallel")),
        name="matmul_residual",
    )(a1, a2, b, x, gate.reshape(ngroups, 1, n))


def _rope_half(y, cos2, sin2):
    return y * cos2 + pltpu.roll(y, GQA_HEAD_DIM // 2, 1) * sin2


def _rope_quarter(v, cosb, sb1, sb2):
    return v * cosb + pltpu.roll(v, 96, 1) * sb1 + pltpu.roll(v, 32, 1) * sb2


def _rms(x, g):
    return x * lax.rsqrt(jnp.mean(x * x, axis=-1, keepdims=True) + EPS) * g


def _prep_a_body(qa_ref, ka_ref, cq_ref, ckv_ref, kr_ref, cosa_ref, sina_ref, cosb_ref, sb1_ref, sb2_ref,
                 qn_ref, kn_ref, cqg_ref, ckvg_ref, qg_ref, kg_ref, cqn_ref, ckvn_ref, krp_ref):
    cosa, sina = cosa_ref[...], sina_ref[...]
    hd = GQA_HEAD_DIM
    q_scale = hd ** -0.5 * LOG2E
    for h in range(GQA_HEADS):
        y = _rms(qa_ref[:, h * hd:(h + 1) * hd].astype(F32), qn_ref[...])
        qg_ref[:, h * hd:(h + 1) * hd] = (_rope_half(y, cosa, sina) * q_scale).astype(BF16)
    for h in range(GQA_KV_HEADS):
        y = _rms(ka_ref[:, h * hd:(h + 1) * hd].astype(F32), kn_ref[...])
        kg_ref[:, h * hd:(h + 1) * hd] = _rope_half(y, cosa, sina).astype(BF16)
    cqn_ref[...] = _rms(cq_ref[...].astype(F32), cqg_ref[...]).astype(BF16)
    ckvn_ref[...] = _rms(ckv_ref[...].astype(F32), ckvg_ref[...]).astype(BF16)
    krp_ref[...] = _rope_quarter(kr_ref[...].astype(F32), cosb_ref[...], sb1_ref[...], sb2_ref[...]).astype(BF16)


def attn_prep(p, tabs, tab_rows, qn_g, kn_g, cq_g, ckv_g):
    r = p.shape[0]
    tm = 256
    tab_tiles = tab_rows // tm
    rowmap = lambda c: (lambda i: (i, c))
    tabmap = lambda i: (i % tab_tiles, 0)
    vec = lambda n: pl.BlockSpec((1, n), lambda i: (0, 0))
    outs = pl.pallas_call(
        _prep_a_body,
        grid=(r // tm,),
        in_specs=[pl.BlockSpec((tm, 2048), rowmap(ATT_QA // 2048)),
                  pl.BlockSpec((tm, 512), rowmap(ATT_KA // 512)),
                  pl.BlockSpec((tm, 1024), rowmap(ATT_CQ // 1024)),
                  pl.BlockSpec((tm, 512), rowmap(ATT_CKV // 512)),
                  pl.BlockSpec((tm, 128), rowmap(ATT_KR // 128)),
                  ] + [pl.BlockSpec((tm, 128), tabmap)] * 5 + [vec(128), vec(128), vec(1024), vec(512)],
        out_specs=[pl.BlockSpec((tm, 2048), lambda i: (i, 0)),
                   pl.BlockSpec((tm, 512), lambda i: (i, 0)),
                   pl.BlockSpec((tm, 1024), lambda i: (i, 0)),
                   pl.BlockSpec((tm, 512), lambda i: (i, 0)),
                   pl.BlockSpec((tm, 128), lambda i: (i, 0))],
        out_shape=[jax.ShapeDtypeStruct((r, 2048), BF16), jax.ShapeDtypeStruct((r, 512), BF16),
                   jax.ShapeDtypeStruct((r, 1024), BF16), jax.ShapeDtypeStruct((r, 512), BF16),
                   jax.ShapeDtypeStruct((r, 128), BF16)],
        compiler_params=_cparams(("parallel",)),
        name="attn_prep",
    )(p, p, p, p, p, *tabs, qn_g.reshape(1, -1), kn_g.reshape(1, -1), cq_g.reshape(1, -1), ckv_g.reshape(1, -1))
    return outs


def _mlaq_body(a_ref, b_ref, cosb_ref, sb1_ref, sb2_ref, o_ref):
    acc = jnp.dot(a_ref[...], b_ref[...], preferred_element_type=F32)
    scale = (MLA_NOPE + MLA_ROPE) ** -0.5 * LOG2E
    cosb, sb1, sb2 = cosb_ref[...], sb1_ref[...], sb2_ref[...]
    for h in range(acc.shape[1] // MLA_QK_PAD):
        c0 = h * MLA_QK_PAD
        o_ref[:, c0:c0 + 128] = (acc[:, c0:c0 + 128] * scale).astype(BF16)
        o_ref[:, c0 + 128:c0 + 256] = (_rope_quarter(acc[:, c0 + 128:c0 + 256], cosb, sb1, sb2) * scale).astype(BF16)


def mla_q(cqn, w_uq_pad, tabs_b, tab_rows, tm):
    m, k = cqn.shape
    n = w_uq_pad.shape[1]
    tn = 1024
    tab_tiles = tab_rows // tm
    return pl.pallas_call(
        _mlaq_body,
        grid=(n // tn, m // tm),
        in_specs=[pl.BlockSpec((tm, k), lambda j, i: (i, 0)),
                  pl.BlockSpec((k, tn), lambda j, i: (0, j))] +
                 [pl.BlockSpec((tm, 128), lambda j, i: (i % tab_tiles, 0))] * 3,
        out_specs=pl.BlockSpec((tm, tn), lambda j, i: (i, j)),
        out_shape=jax.ShapeDtypeStruct((m, n), BF16),
        compiler_params=_cparams(("parallel", "parallel")),
        name="mla_q",
    )(cqn, w_uq_pad, *tabs_b)


def _flash_body(*refs, has_lat, kparts, group, tn, tk, s_lat):
    idx = 1
    q_ref = refs[0]
    if has_lat:
        klat = refs[idx:idx + kparts]
        vlat = refs[idx + kparts]
        idx += kparts + 1
    kctx = refs[idx:idx + kparts]
    vctx = refs[idx + kparts]
    idx += kparts + 1
    g_ref, o_ref, m_ref, l_ref, acc_ref, s_ref = refs[idx:idx + 6]

    dv = acc_ref.shape[0]
    ntiles = acc_ref.shape[1] // tn
    dq = q_ref.shape[1] // group
    n_ctx = vctx.shape[0]

    def q_tile(j):
        return q_ref[:, j * dq:(j + 1) * dq] if group > 1 else q_ref[j * tn:(j + 1) * tn, :]

    m_ref[...] = jnp.full(m_ref.shape, -1e30, F32)
    l_ref[...] = jnp.zeros(l_ref.shape, F32)
    acc_ref[...] = jnp.zeros(acc_ref.shape, F32)

    def cat(parts):
        return parts[0] if len(parts) == 1 else jnp.concatenate(parts, axis=1)

    def produce(slot, krefs, off, nk):
        rows = slice(None) if off is None else pl.ds(off, nk)
        kc = cat([r[rows, :] for r in krefs])
        for j in range(ntiles):
            s_ref[slot, :nk, j * tn:(j + 1) * tn] = lax.dot_general(
                kc, q_tile(j), (((1,), (1,)), ((), ())), preferred_element_type=F32)

    def consume(slot, vref, off, nk):
        vc = vref[...] if off is None else vref[pl.ds(off, nk), :]
        for j in range(ntiles):
            sl = slice(j * tn, (j + 1) * tn)
            s = s_ref[slot, :nk, sl]
            m_prev = m_ref[:, sl]
            m_new = jnp.maximum(m_prev, jnp.max(s, axis=0, keepdims=True))
            alpha = jnp.exp2(m_prev - m_new)
            p = jnp.exp2(s - m_new)
            l_ref[:, sl] = alpha * l_ref[:, sl] + jnp.sum(p, axis=0, keepdims=True)
            pv = lax.dot_general(vc, p.astype(BF16), (((0,), (0,)), ((), ())), preferred_element_type=F32)
            acc_ref[:, sl] = alpha * acc_ref[:, sl] + pv
            m_ref[:, sl] = m_new

    if has_lat:
        nchunks = s_lat // tk
        assert nchunks % 2 == 0 and nchunks >= 2
        produce(0, klat, 0, tk)

        def pair(i, carry):
            c0 = pl.multiple_of(2 * i * tk, tk)
            c1 = pl.multiple_of(c0 + tk, tk)
            c2 = pl.multiple_of(c0 + 2 * tk, tk)
            produce(1, klat, c1, tk)
            consume(0, vlat, c0, tk)
            produce(0, klat, c2, tk)
            consume(1, vlat, c1, tk)
            return carry

        lax.fori_loop(0, nchunks // 2 - 1, pair, 0)
        last = (nchunks - 2) * tk
        produce(1, klat, last + tk, tk)
        consume(0, vlat, last, tk)
        produce(0, kctx, None, n_ctx)
        consume(1, vlat, last + tk, tk)
    else:
        produce(0, kctx, None, n_ctx)
    consume(0, vctx, None, n_ctx)

    for j in range(ntiles):
        sl = slice(j * tn, (j + 1) * tn)
        o = jnp.transpose(acc_ref[:, sl] / l_ref[:, sl])
        if group > 1:
            gt = g_ref[:, j * dv:(j + 1) * dv].astype(F32)
            o_ref[:, j * dv:(j + 1) * dv] = (o * _silu(gt)).astype(BF16)
        else:
            gt = g_ref[sl, :].astype(F32)
            o_ref[sl, :] = (o * _silu(gt)).astype(BF16)


def flash(q, klat, vlat, kctx, vctx, gate, *, batch, heads, group, dqk, s_lat, s_ctx, tq, q_is_ctx):
    dv = 128
    kparts = len(kctx)
    has_lat = not q_is_ctx
    nq = (s_ctx if q_is_ctx else s_lat) // tq
    tk = FLASH_TK
    qmap = lambda b, h, i: (b * nq + i, h)
    in_specs = [pl.BlockSpec((tq, group * dqk), qmap)]
    args = [q]
    if has_lat:
        for arr, c0 in klat:
            in_specs.append(pl.BlockSpec((s_lat, 128), functools.partial(lambda b, h, i, c0, per: (b, c0 + per * h), c0=c0[0], per=c0[1])))
            args.append(arr)
        in_specs.append(pl.BlockSpec((s_lat, dv), lambda b, h, i: (b, vlat[1] + h)))
        args.append(vlat[0])
    for arr, c0 in kctx:
        in_specs.append(pl.BlockSpec((s_ctx, 128), functools.partial(lambda b, h, i, c0, per: (b, c0 + per * h), c0=c0[0], per=c0[1])))
        args.append(arr)
    in_specs.append(pl.BlockSpec((s_ctx, dv), lambda b, h, i: (b, vctx[1] + h)))
    args.append(vctx[0])
    in_specs.append(pl.BlockSpec((tq, group * dv), lambda b, h, i: (b * nq + i, gate[1] + h)))
    args.append(gate[0])
    rows = q.shape[0]
    tn = min(FLASH_TN, tq)
    assert tq % tn == 0 and (group == 1 or tq == tn)
    assert s_ctx <= tk
    body = functools.partial(_flash_body, has_lat=has_lat, kparts=kparts, group=group, tn=tn, tk=tk, s_lat=s_lat)
    return pl.pallas_call(
        body,
        grid=(batch, heads, nq),
        in_specs=in_specs,
        out_specs=pl.BlockSpec((tq, group * dv), qmap),
        out_shape=jax.ShapeDtypeStruct((rows, heads * group * dv), BF16),
        scratch_shapes=[pltpu.VMEM((1, group * tq), F32), pltpu.VMEM((1, group * tq), F32),
                        pltpu.VMEM((dv, group * tq), F32), pltpu.VMEM((2, tk, group * tq), F32)],
        compiler_params=_cparams(("parallel", "parallel", "parallel")),
        name="flash_ctx" if q_is_ctx else "flash_lat",
    )(*args)


def _gla_chunk(q_ref, k_ref, v_ref, gd_ref, wg_ref, bg_ref, st_ref, out_ref, rev):
    c, sub = GLA_CHUNK, GLA_SUB
    nt = (((1,), (1,)), ((), ()))
    v = v_ref[...]
    kf = k_ref[...].astype(F32)
    z = jnp.dot(gd_ref[...].astype(BF16), wg_ref[0], preferred_element_type=F32) + bg_ref[0]
    lg = (jnp.minimum(z, 0.0) - jnp.log(1.0 + jnp.exp(-jnp.abs(z)))) * (LOG2E / GLA_GATE_TAU)
    row = lax.broadcasted_iota(jnp.int32, (c, c), 0)
    col = lax.broadcasted_iota(jnp.int32, (c, c), 1)
    tri = jnp.where((row <= col) if rev else (row >= col), 1.0, 0.0).astype(F32)
    cum = jnp.dot(tri, lg, preferred_element_type=F32, precision=lax.Precision.HIGHEST)
    tot = cum[0:1] if rev else cum[c - 1:c]
    st = st_ref[...]
    yield

    if q_ref is not None:
        qf = q_ref[...].astype(F32) * (GLA_DK ** -0.5)
        lane = col[0:1]

        def level(blk):
            qparts, kparts = [], []
            zero = jnp.zeros((blk, GLA_DK), BF16)
            for a in range(0, c, 2 * blk):
                if rev:
                    i0, j0, cref = a, a + blk, cum[a + blk:a + blk + 1]
                else:
                    i0, j0, cref = a + blk, a, cum[a + blk - 1:a + blk]
                qi = (qf[i0:i0 + blk] * jnp.exp2(cum[i0:i0 + blk] - cref)).astype(BF16)
                kj = (kf[j0:j0 + blk] * jnp.exp2(cref - cum[j0:j0 + blk])).astype(BF16)
                qparts += [qi, zero] if rev else [zero, qi]
                kparts += [zero, kj] if rev else [kj, zero]
            a_l = lax.dot_general(jnp.concatenate(qparts, axis=0), jnp.concatenate(kparts, axis=0), nt,
                                  preferred_element_type=F32)
            sh = (2 * blk).bit_length() - 1
            return jnp.where(jnp.right_shift(row, sh) == jnp.right_shift(col, sh), a_l, 0.0)

        def diag(a):
            qs, ks, cs = qf[a:a + sub], kf[a:a + sub], cum[a:a + sub]
            att = jnp.zeros((sub, c), F32)
            for j in range(sub):
                e = jnp.exp2(cs - cs[j:j + 1])
                colv = jnp.sum(qs * e * ks[j:j + 1], axis=-1, keepdims=True)
                att = jnp.where(lane == a + j, colv, att)
            return att

        blocks = list(range(0, c, sub))
        levels, drows = [], []
        blk = c // 2
        while blk >= sub:
            levels.append(level(blk))
            take = max(1, len(blocks) // 2) if blk > sub else len(blocks)
            drows += [diag(a) for a in blocks[:take]]
            blocks = blocks[take:]
            blk //= 2
        in_order = (col >= row) if rev else (col <= row)
        sh = sub.bit_length() - 1
        same = jnp.right_shift(row, sh) == jnp.right_shift(col, sh)
        att = jnp.where(same & in_order, jnp.concatenate(drows, axis=0), 0.0)
        for a_l in levels:
            att = att + a_l
        yield
        qe = (qf * jnp.exp2(cum)).astype(BF16)
        o = lax.dot_general(qe, st.astype(BF16), nt, preferred_element_type=F32)
        o = o + jnp.dot(att.astype(BF16), v, preferred_element_type=F32)
        out_ref[...] = o.astype(out_ref.dtype)

    kt = (kf * jnp.exp2(tot - cum)).astype(BF16)
    st_ref[...] = st * jnp.exp2(tot) + lax.dot_general(v, kt, (((0,), (0,)), ((), ())), preferred_element_type=F32)


def _gla_body(*refs, with_out, has_init, nsteps):
    per_dir = (4 if with_out else 3) + 2 + (1 if has_init else 0)
    ins = [refs[d * per_dir:(d + 1) * per_dir] for d in range(2)]
    outs = refs[2 * per_dir:2 * per_dir + 2]
    scratch = refs[2 * per_dir + 2:]
    step = pl.program_id(2)

    @pl.when(step == 0)
    def _():
        for d in range(2):
            scratch[d][...] = ins[d][-1][0, 0] if has_init else jnp.zeros(scratch[d].shape, F32)

    chains = []
    for d, rev in enumerate((False, True)):
        r = list(ins[d])
        q_ref = r.pop(0) if with_out else None
        k_ref, v_ref, gd_ref, wg_ref, bg_ref = r[:5]
        chains.append(_gla_chunk(q_ref, k_ref, v_ref, gd_ref, wg_ref, bg_ref, scratch[d], outs[d], rev))
    for _ in itertools.zip_longest(*chains):
        pass

    if not with_out:
        @pl.when(step == nsteps - 1)
        def _():
            for d in range(2):
                outs[d][0, 0] = scratch[d][...]


def gla(q, k, v, gd, wgs, bgs, s0s, *, batch, rows_per_batch):
    c = GLA_CHUNK
    nsteps = rows_per_batch // c
    with_out = q is not None
    has_init = s0s is not None

    def rmap(off, rev, per_head=1):
        if rev:
            return lambda b, h, s: (b * nsteps + nsteps - 1 - s, off + per_head * h)
        return lambda b, h, s: (b * nsteps + s, off + per_head * h)

    in_specs, args = [], []
    for d, rev in enumerate((False, True)):
        if with_out:
            in_specs.append(pl.BlockSpec((c, GLA_DK), rmap(q[1], rev)))
            args.append(q[0])
        in_specs += [pl.BlockSpec((c, GLA_DK), rmap(k[1], rev)),
                     pl.BlockSpec((c, GLA_DV), rmap(v[1], rev)),
                     pl.BlockSpec((c, 128), rmap(gd[1], rev, 0)),
                     pl.BlockSpec((1, 128, GLA_DK), lambda b, h, s: (h, 0, 0)),
                     pl.BlockSpec((1, 1, GLA_DK), lambda b, h, s: (h, 0, 0))]
        args += [k[0], v[0], gd[0], wgs[d], bgs[d]]
        if has_init:
            in_specs.append(pl.BlockSpec((1, 1, GLA_DV, GLA_DK), lambda b, h, s: (b, h, 0, 0)))
            args.append(s0s[d])
    scratch = [pltpu.VMEM((GLA_DV, GLA_DK), F32)] * 2
    if with_out:
        out_specs = [pl.BlockSpec((c, GLA_DV), rmap(0, rev)) for rev in (False, True)]
        out_shape = [jax.ShapeDtypeStruct((batch * rows_per_batch, GLA_HEADS * GLA_DV), BF16)] * 2
    else:
        out_specs = [pl.BlockSpec((1, 1, GLA_DV, GLA_DK), lambda b, h, s: (b, h, 0, 0))] * 2
        out_shape = [jax.ShapeDtypeStruct((batch, GLA_HEADS, GLA_DV, GLA_DK), F32)] * 2
    body = functools.partial(_gla_body, with_out=with_out, has_init=has_init, nsteps=nsteps)
    return pl.pallas_call(
        body,
        grid=(batch, GLA_HEADS, nsteps),
        in_specs=in_specs,
        out_specs=out_specs,
        out_shape=out_shape,
        scratch_shapes=scratch,
        compiler_params=_cparams(("parallel", "parallel", "arbitrary")),
        name="gla_scan" if with_out else "gla_state",
    )(*args)


def _fft_a_body(x_ref, gc_ref, gs_ref, tr_ref, ti_ref, o_ref):
    n1, t2, ch = x_ref.shape
    x = x_ref[...].reshape(n1 * t2, ch)
    yr = jnp.dot(gc_ref[...], x, preferred_element_type=F32)
    yi = -jnp.dot(gs_ref[...], x, preferred_element_type=F32)
    tr, ti = tr_ref[...], ti_ref[...]
    o_ref[:, :, :ch] = (yr * tr - yi * ti).astype(BF16).reshape(n1, t2, ch)
    o_ref[:, :, ch:] = (yr * ti + yi * tr).astype(BF16).reshape(n1, t2, ch)


def _fft_b_body(y_ref, fc_ref, fs_ref, cc_ref, sc_ref, o_ref, *, norm):
    ch = y_ref.shape[1] // 2
    yr, yi = y_ref[:, :ch], y_ref[:, ch:]
    fc, fs = fc_ref[...], fs_ref[...]
    zr = jnp.dot(fc, yr, preferred_element_type=F32) + jnp.dot(fs, yi, preferred_element_type=F32)
    zi = jnp.dot(fc, yi, preferred_element_type=F32) - jnp.dot(fs, yr, preferred_element_type=F32)
    for g in range(FNET_GROUPS):
        sl = slice(g * FNET_DIM, (g + 1) * FNET_DIM)
        f = jnp.dot(zr[:, sl].astype(BF16), cc_ref[...], preferred_element_type=F32)
        f = f + jnp.dot(zi[:, sl].astype(BF16), sc_ref[...], preferred_element_type=F32)
        o_ref[:, sl] = (f * norm).astype(BF16)


def _dft_tables(s):
    n1, n2, t2 = FFT_N1, s // FFT_N1, FFT_T2
    a = np.arange(n1)
    ang1 = 2.0 * np.pi * ((a[:, None] * a[None, :]) % n1) / n1
    eye = np.eye(t2)
    gc = np.kron(np.cos(ang1), eye)
    gs = np.kron(np.sin(ang1), eye)
    b = np.arange(n2)
    angt = 2.0 * np.pi * ((a[:, None] * b[None, :]) % s) / s
    tw = angt.reshape(n1, n2 // t2, t2).transpose(1, 0, 2).reshape(n2 // t2, n1 * t2, 1)
    ang2 = 2.0 * np.pi * ((b[:, None] * b[None, :]) % n2) / n2
    d = np.arange(FNET_DIM)
    angc = 2.0 * np.pi * ((d[:, None] * d[None, :]) % FNET_DIM) / FNET_DIM
    bf = lambda m: jnp.asarray(m, dtype=BF16)
    return dict(gc=bf(gc), gs=bf(gs), tr=jnp.asarray(np.cos(tw), F32), ti=jnp.asarray(-np.sin(tw), F32),
                fc=bf(np.cos(ang2)), fs=bf(np.sin(ang2)), cc=bf(np.cos(angc)), sc=bf(np.sin(angc)))


def fourier_mix(p, u_col, batch, s):
    n1, n2, t2 = FFT_N1, s // FFT_N1, FFT_T2
    ch = FNET_GROUPS * FNET_DIM
    tabs = _dft_tables(s)
    p4 = p.reshape(batch, n1, n2, p.shape[1])
    full = lambda shape: pl.BlockSpec(shape, lambda b, j: (0,) * len(shape))
    y = pl.pallas_call(
        _fft_a_body,
        grid=(batch, n2 // t2),
        in_specs=[pl.BlockSpec((None, n1, t2, ch), lambda b, j: (b, 0, j, u_col // ch)),
                  full((n1 * t2, n1 * t2)), full((n1 * t2, n1 * t2)),
                  pl.BlockSpec((None, n1 * t2, 1), lambda b, j: (j, 0, 0)),
                  pl.BlockSpec((None, n1 * t2, 1), lambda b, j: (j, 0, 0))],
        out_specs=pl.BlockSpec((None, n1, t2, 2 * ch), lambda b, j: (b, 0, j, 0)),
        out_shape=jax.ShapeDtypeStruct((batch, n1, n2, 2 * ch), BF16),
        compiler_params=_cparams(("parallel", "parallel")),
        name="fft_stage_a",
    )(p4, tabs["gc"], tabs["gs"], tabs["tr"], tabs["ti"])
    norm = float(1.0 / np.sqrt(s * FNET_DIM))
    f = pl.pallas_call(
        functools.partial(_fft_b_body, norm=norm),
        grid=(batch, n1),
        in_specs=[pl.BlockSpec((None, None, n2, 2 * ch), lambda b, j: (b, j, 0, 0)),
                  full((n2, n2)), full((n2, n2)), full((FNET_DIM, FNET_DIM)), full((FNET_DIM, FNET_DIM))],
        out_specs=pl.BlockSpec((None, None, n2, ch), lambda b, j: (b, j, 0, 0)),
        out_shape=jax.ShapeDtypeStruct((batch, n1, n2, ch), BF16),
        compiler_params=_cparams(("parallel", "parallel")),
        name="fft_stage_b",
    )(y, tabs["fc"], tabs["fs"], tabs["cc"], tabs["sc"])
    return f.transpose(0, 2, 1, 3).reshape(batch * s, ch)


def _finish_body(of_ref, ob_ref, f_ref, g_ref, ong_ref, o_ref):
    dv = GLA_DV
    for h in range(GLA_HEADS):
        sl = slice(h * dv, (h + 1) * dv)
        o = of_ref[:, sl].astype(F32) + ob_ref[:, sl].astype(F32)
        y = _rms(o, ong_ref[...])
        gt = g_ref[:, sl].astype(F32)
        o_ref[:, sl] = (y * _silu(gt)).astype(BF16)
    base = GLA_HEADS * dv
    gt = g_ref[:, base:].astype(F32)
    o_ref[:, base:] = (f_ref[...].astype(F32) * _silu(gt)).astype(BF16)


def rec_finish(o_f, o_b, f, p, on_g):
    r = o_f.shape[0]
    tm = 256
    w = GLA_HEADS * GLA_DV
    mix = w + FNET_GROUPS * FNET_DIM
    return pl.pallas_call(
        _finish_body,
        grid=(r // tm,),
        in_specs=[pl.BlockSpec((tm, w), lambda i: (i, 0)),
                  pl.BlockSpec((tm, w), lambda i: (i, 0)),
                  pl.BlockSpec((tm, mix - w), lambda i: (i, 0)),
                  pl.BlockSpec((tm, mix), lambda i: (i, REC_GATE // mix)),
                  pl.BlockSpec((1, GLA_DV), lambda i: (0, 0))],
        out_specs=pl.BlockSpec((tm, mix), lambda i: (i, 0)),
        out_shape=jax.ShapeDtypeStruct((r, mix), BF16),
        compiler_params=_cparams(("parallel",)),
        name="rec_finish",
    )(o_f, o_b, f, p, on_g.reshape(1, -1))


def _rope_tables(s):
    t = np.arange(s)
    row, col = (t // GRID_W).astype(np.float32), (t % GRID_W).astype(np.float32)

    def cs(rot_dim):
        quarter = rot_dim // 4
        inv_freq = (np.float32(ROPE_THETA) ** (-np.arange(quarter, dtype=np.float32) / quarter)).astype(np.float32)
        ang = np.concatenate([row[:, None] * inv_freq, col[:, None] * inv_freq], axis=-1).astype(np.float32)
        return np.cos(ang), np.sin(ang)

    ca, sa = cs(GQA_HEAD_DIM)
    cb, sb = cs(MLA_ROPE)
    z32, z64 = np.zeros((s, 32), np.float32), np.zeros((s, 64), np.float32)
    tabs = [np.concatenate([ca, ca], 1), np.concatenate([-sa, sa], 1),
            np.concatenate([cb, cb, z64], 1), np.concatenate([-sb, z32, z64], 1), np.concatenate([z32, sb, z64], 1)]
    return [jnp.asarray(t_, F32) for t_ in tabs]


def _identity_rope_tables(n):
    one, zero = np.ones((n, 128), np.float32), np.zeros((n, 128), np.float32)
    cb = np.concatenate([np.ones((n, 64), np.float32), np.zeros((n, 64), np.float32)], 1)
    return [jnp.asarray(t_, F32) for t_ in (one, zero, cb, zero, zero)]


def attention_calls(p_lat, p_ctx, qg_l, qg_c, kg_l, kg_c, qm_l, qm_c, kv_l, kv_c, krp_l, krp_c, batch, s, n_ctx):
    va_col, gate_a_col, gate_b_col = ATT_VA // 128, ATT_GATE // 512, (ATT_GATE + 2048) // 128
    common = dict(batch=batch, s_lat=s, s_ctx=n_ctx)
    gqa = dict(heads=GQA_KV_HEADS, group=GQA_GROUP, dqk=GQA_HEAD_DIM, **common)
    mla = dict(heads=MLA_HEADS, group=1, dqk=MLA_QK_PAD, **common)
    kb_l, kb_c = [(kv_l, (0, 1)), (krp_l, (0, 0))], [(kv_c, (0, 1)), (krp_c, (0, 0))]
    ya_l = flash(qg_l, [(kg_l, (0, 1))], (p_lat, va_col), [(kg_c, (0, 1))], (p_ctx, va_col), (p_lat, gate_a_col),
                 tq=FLASH_TN, q_is_ctx=False, **gqa)
    yb_l = flash(qm_l, kb_l, (kv_l, MLA_HEADS), kb_c, (kv_c, MLA_HEADS), (p_lat, gate_b_col),
                 tq=2 * FLASH_TN, q_is_ctx=False, **mla)
    ya_c = flash(qg_c, None, None, [(kg_c, (0, 1))], (p_ctx, va_col), (p_ctx, gate_a_col),
                 tq=n_ctx, q_is_ctx=True, **gqa)
    yb_c = flash(qm_c, None, None, kb_c, (kv_c, MLA_HEADS), (p_ctx, gate_b_col), tq=n_ctx, q_is_ctx=True, **mla)
    return ya_l, ya_c, yb_l, yb_c


def _attention_layer(x_lat, x_ctx, mod, norm_g, w_in, qn_g, kn_g, cq_g, ckv_g, w_uq, w_ukv, w_out, batch, s, n_ctx):
    d = x_lat.shape[1]
    shift, scale, gate = mod[:, :d], mod[:, d:2 * d], mod[:, 2 * d:]
    h_lat = modnorm(x_lat, norm_g, scale[:batch], shift[:batch], s, BF16)
    h_ctx = modnorm(x_ctx, norm_g, scale[batch:batch + 1], shift[batch:batch + 1], batch * n_ctx, BF16)

    w_in_p = jnp.concatenate([w_in[:, :4672], jnp.zeros((d, ATT_GATE - 4672), w_in.dtype), w_in[:, 4672:]], axis=1).astype(BF16)
    w_uq_p = jnp.pad(w_uq.reshape(MLA_Q_LORA, MLA_HEADS, MLA_NOPE + MLA_ROPE),
                     ((0, 0), (0, 0), (0, MLA_QK_PAD - MLA_NOPE - MLA_ROPE))).reshape(MLA_Q_LORA, MLA_HEADS * MLA_QK_PAD).astype(BF16)
    w_ukv3 = w_ukv.reshape(MLA_KV_LORA, MLA_HEADS, MLA_NOPE + MLA_V)
    w_ukv_p = jnp.concatenate([w_ukv3[:, :, :MLA_NOPE].reshape(MLA_KV_LORA, -1),
                               w_ukv3[:, :, MLA_NOPE:].reshape(MLA_KV_LORA, -1)], axis=1).astype(BF16)
    w_out_b = w_out.astype(BF16)

    ctx_rows = batch * n_ctx
    p_lat = matmul(h_lat, w_in_p, 1024, 1024, BF16)
    p_ctx = matmul(h_ctx, w_in_p, ctx_rows, 1024, BF16)

    tabs_lat, tabs_ctx = _rope_tables(s), _identity_rope_tables(ctx_rows)
    qg_l, kg_l, cqn_l, ckvn_l, krp_l = attn_prep(p_lat, tabs_lat, s, qn_g, kn_g, cq_g, ckv_g)
    qg_c, kg_c, cqn_c, ckvn_c, krp_c = attn_prep(p_ctx, tabs_ctx, ctx_rows, qn_g, kn_g, cq_g, ckv_g)
    qm_l = mla_q(cqn_l, w_uq_p, tabs_lat[2:], s, 1024)
    qm_c = mla_q(cqn_c, w_uq_p, tabs_ctx[2:], ctx_rows, ctx_rows)
    kv_l = matmul(ckvn_l, w_ukv_p, 1024, 1024, BF16)
    kv_c = matmul(ckvn_c, w_ukv_p, ctx_rows, 1024, BF16)

    ya_l, ya_c, yb_l, yb_c = attention_calls(p_lat, p_ctx, qg_l, qg_c, kg_l, kg_c, qm_l, qm_c, kv_l, kv_c, krp_l, krp_c,
                                             batch, s, n_ctx)

    x_lat = matmul_residual(ya_l, yb_l, w_out_b, x_lat, gate[:batch], s, 1024, 512)
    x_ctx = matmul_residual(ya_c, yb_c, w_out_b, x_ctx, gate[batch:batch + 1], ctx_rows, ctx_rows, 512)
    return x_lat, x_ctx


def _recurrent_layer(x_lat, x_ctx, mod, norm_g, w_in, wg_f, bg_f, wg_b, bg_b, on_g, w_out, batch, s, n_ctx):
    d = x_lat.shape[1]
    shift, scale, gate = mod[:, :d], mod[:, d:2 * d], mod[:, 2 * d:]
    h_lat = modnorm(x_lat, norm_g, scale[:batch], shift[:batch], s, BF16)
    h_ctx = modnorm(x_ctx, norm_g, scale[batch:batch + 1], shift[batch:batch + 1], batch * n_ctx, BF16)

    w_main = jnp.concatenate([w_in[:, 7200:], w_in[:, :6144], w_in[:, 6176:7200]], axis=1).astype(BF16)
    w_gd = jnp.pad(w_in[:, 6144:6176], ((0, 0), (0, 96))).astype(BF16)
    w_out_b = w_out.astype(BF16)

    def gate_w(wg, row0):
        w3 = wg.reshape(GLA_GATE_RANK, GLA_HEADS, GLA_DK).transpose(1, 0, 2)
        return jnp.pad(w3, ((0, 0), (row0, 128 - GLA_GATE_RANK - row0), (0, 0))).astype(BF16)

    wgf_p, wgb_p = gate_w(wg_f, 0), gate_w(wg_b, GLA_GATE_RANK)
    bgf, bgb = bg_f.reshape(GLA_HEADS, 1, GLA_DK), bg_b.reshape(GLA_HEADS, 1, GLA_DK)

    ctx_rows = batch * n_ctx
    p_lat = matmul(h_lat, w_main, 1024, 1024, BF16)
    p_ctx = matmul(h_ctx, w_main, ctx_rows, 1024, BF16)
    gd_lat = matmul(h_lat, w_gd, 1024, 128, F32)
    gd_ctx = matmul(h_ctx, w_gd, ctx_rows, 128, F32)

    kq, kk, kv = REC_Q // GLA_DK, REC_K // GLA_DK, REC_V // GLA_DV
    wgs, bgs = (wgf_p, wgb_p), (bgf, bgb)
    states = gla(None, (p_ctx, kk), (p_ctx, kv), (gd_ctx, 0), wgs, bgs, None, batch=batch, rows_per_batch=n_ctx)
    o_f, o_b = gla((p_lat, kq), (p_lat, kk), (p_lat, kv), (gd_lat, 0), wgs, bgs, states, batch=batch, rows_per_batch=s)

    f = fourier_mix(p_lat, REC_U, batch, s)
    y = rec_finish(o_f, o_b, f, p_lat, on_g)
    return matmul_residual_single(y, w_out_b, x_lat, gate[:batch], s, 1024, 512)


def _mm_res1_body(a_ref, b_ref, x_ref, g_ref, o_ref):
    acc = jnp.dot(a_ref[...], b_ref[...], preferred_element_type=F32)
    o_ref[...] = x_ref[...] + g_ref[0] * acc


def matmul_residual_single(a, b, x, gate, rows_per_group, tm, tn):
    m, k = a.shape
    n = b.shape[1]
    ngroups = gate.shape[0]
    tiles_per_group = rows_per_group // tm
    return pl.pallas_call(
        _mm_res1_body,
        grid=(n // tn, m // tm),
        in_specs=[pl.BlockSpec((tm, k), lambda j, i: (i, 0)),
                  pl.BlockSpec((k, tn), lambda j, i: (0, j)),
                  pl.BlockSpec((tm, tn), lambda j, i: (i, j)),
                  pl.BlockSpec((1, 1, tn), lambda j, i: (jnp.minimum(i // tiles_per_group, ngroups - 1), 0, j))],
        out_specs=pl.BlockSpec((tm, tn), lambda j, i: (i, j)),
        out_shape=jax.ShapeDtypeStruct((m, n), F32),
        compiler_params=_cparams(("parallel", "parallel")),
        name="matmul_residual1",
    )(a, b, x, gate.reshape(ngroups, 1, n))


def kernel(x, c, ctx, c_ctx, norm_g, ada_w, ada_b, att_w_in, att_qn_g, att_kn_g, mla_cq_g, mla_ckv_g, mla_w_uq,
           mla_w_ukv, att_w_out, rec_w_in, gla_wg_f, gla_bg_f, gla_wg_b, gla_bg_b, gla_on_g, rec_w_out, final_g):
    batch, s, d = x.shape
    n_ctx = ctx.shape[1]
    x_lat = x.reshape(batch * s, d)
    x_ctx = ctx.reshape(batch * n_ctx, d)
    cond = jnp.concatenate([c, c_ctx[None, :], jnp.zeros((8 - batch - 1, d), F32)], axis=0)
    mods = adaln(cond, ada_w, ada_b)

    x_lat, x_ctx = _attention_layer(x_lat, x_ctx, mods[0], norm_g[0], att_w_in[0], att_qn_g[0], att_kn_g[0],
                                    mla_cq_g[0], mla_ckv_g[0], mla_w_uq[0], mla_w_ukv[0], att_w_out[0],
                                    batch, s, n_ctx)
    x_lat = _recurrent_layer(x_lat, x_ctx, mods[1], norm_g[1], rec_w_in[0], gla_wg_f[0], gla_bg_f[0], gla_wg_b[0],
                             gla_bg_b[0], gla_on_g[0], rec_w_out[0], batch, s, n_ctx)
    zero = jnp.zeros((1, d), F32)
    out = modnorm(x_lat, final_g, zero, zero, batch * s, F32)
    return out.reshape(batch, s, d)
```

```python
import functools
import itertools

import numpy as np
import jax
import jax.numpy as jnp
from jax import lax
from jax.experimental import pallas as pl
from jax.experimental.pallas import tpu as pltpu

F32 = jnp.float32
BF16 = jnp.bfloat16

EPS = 1e-6
GRID_W = 64
ROPE_THETA = 10000.0

GQA_HEADS = 16
GQA_KV_HEADS = 4
GQA_GROUP = GQA_HEADS // GQA_KV_HEADS
GQA_HEAD_DIM = 128
MLA_HEADS = 16
MLA_Q_LORA = 1024
MLA_KV_LORA = 512
MLA_NOPE = 128
MLA_ROPE = 64
MLA_V = 128
MLA_QK_PAD = 256

GLA_HEADS = 6
GLA_DK = 256
GLA_DV = 512
GLA_GATE_RANK = 16
GLA_GATE_TAU = 16.0
GLA_CHUNK = 128
GLA_SUB = 8
FNET_GROUPS = 4
FNET_DIM = 256
FLASH_TN = 512
FLASH_TK = 1024
LOG2E = 1.4426950408889634
FFT_N1 = 64
FFT_T2 = 16

ATT_QA, ATT_KA, ATT_VA, ATT_CQ, ATT_CKV, ATT_KR, ATT_MAIN = 0, 2048, 2560, 3072, 4096, 4608, 4608
ATT_T_GATE, ATT_T_KR, ATT_TAIL = 0, 4096, 4608
REC_Q, REC_K, REC_V, REC_MAIN = 0, 1536, 3072, 6144
REC_T_GATE, REC_T_U = 0, 4096

VMEM_LIMIT = 56 * 1024 * 1024


def _cparams(sem):
    return pltpu.CompilerParams(dimension_semantics=sem, vmem_limit_bytes=VMEM_LIMIT)


def _silu(x):
    return x * jax.nn.sigmoid(x)


def _adaln_body(a_ref, w_ref, b_ref, o_ref):
    a = _silu(a_ref[...]).astype(BF16)
    o_ref[0] = jnp.dot(a, w_ref[0].astype(BF16), preferred_element_type=F32) + b_ref[0]


def adaln(cond, ada_w, ada_b):
    depth, d, n = ada_w.shape
    tn = 512
    return pl.pallas_call(
        _adaln_body,
        grid=(depth, n // tn),
        in_specs=[pl.BlockSpec((8, d), lambda l, j: (0, 0)),
                  pl.BlockSpec((1, d, tn), lambda l, j: (l, 0, j)),
                  pl.BlockSpec((1, 1, tn), lambda l, j: (l, 0, j))],
        out_specs=pl.BlockSpec((1, 8, tn), lambda l, j: (l, 0, j)),
        out_shape=jax.ShapeDtypeStruct((depth, 8, n), F32),
        compiler_params=_cparams(("parallel", "parallel")),
        name="adaln",
    )(cond, ada_w, ada_b.reshape(depth, 1, n))


def _modnorm_body(x_ref, g_ref, sc_ref, sh_ref, o_ref):
    x = x_ref[...]
    y = x * lax.rsqrt(jnp.mean(x * x, axis=-1, keepdims=True) + EPS)
    y = y * g_ref[...]
    o_ref[...] = (y * (1.0 + sc_ref[0]) + sh_ref[0]).astype(o_ref.dtype)


def modnorm(x, g, scale, shift, rows_per_group, out_dtype):
    r, d = x.shape
    tm = 256
    ngroups = scale.shape[0]
    tiles_per_group = rows_per_group // tm
    gmap = lambda i: (jnp.minimum(i // tiles_per_group, ngroups - 1), 0, 0)
    return pl.pallas_call(
        _modnorm_body,
        grid=(r // tm,),
        in_specs=[pl.BlockSpec((tm, d), lambda i: (i, 0)),
                  pl.BlockSpec((1, d), lambda i: (0, 0)),
                  pl.BlockSpec((1, 1, d), gmap),
                  pl.BlockSpec((1, 1, d), gmap)],
        out_specs=pl.BlockSpec((tm, d), lambda i: (i, 0)),
        out_shape=jax.ShapeDtypeStruct((r, d), out_dtype),
        compiler_params=_cparams(("parallel",)),
        name="modnorm",
    )(x, g.reshape(1, d), scale.reshape(ngroups, 1, d), shift.reshape(ngroups, 1, d))


def _resident_weight(b_ref, wb):
    if not wb:
        return b_ref

    @pl.when(pl.program_id(1) == 0)
    def _():
        wb[0][...] = b_ref[...].astype(BF16)

    return wb[0]


def _weight_scratch(b, k, tn):
    if b.dtype == BF16:
        return [], ("parallel", "parallel")
    return [pltpu.VMEM((k, tn), BF16)], ("parallel", "arbitrary")


def _mm_body(a_ref, b_ref, o_ref, *wb):
    w = _resident_weight(b_ref, wb)
    o_ref[...] = jnp.dot(a_ref[...], w[...], preferred_element_type=F32).astype(o_ref.dtype)


def matmul(a, b, tm, tn, out_dtype, col0=0, n=None):
    m, k = a.shape
    n = b.shape[1] if n is None else n
    scratch, sem = _weight_scratch(b, k, tn)
    return pl.pallas_call(
        _mm_body,
        grid=(n // tn, m // tm),
        in_specs=[pl.BlockSpec((tm, k), lambda j, i: (i, 0)),
                  pl.BlockSpec((k, tn), lambda j, i: (0, col0 + j))],
        out_specs=pl.BlockSpec((tm, tn), lambda j, i: (i, j)),
        out_shape=jax.ShapeDtypeStruct((m, n), out_dtype),
        scratch_shapes=scratch,
        compiler_params=_cparams(sem),
        name="matmul",
    )(a, b)


def _mm_res_body(a1_ref, a2_ref, b_ref, x_ref, g_ref, o_ref, *wb):
    w = _resident_weight(b_ref, wb)
    k1 = a1_ref.shape[1]
    acc = jnp.dot(a1_ref[...], w[:k1, :], preferred_element_type=F32)
    acc = acc + jnp.dot(a2_ref[...], w[k1:, :], preferred_element_type=F32)
    o_ref[...] = x_ref[...] + g_ref[0] * acc


def matmul_residual(a1, a2, b, x, gate, rows_per_group, tm, tn):
    m, k1 = a1.shape
    k2 = a2.shape[1]
    n = b.shape[1]
    ngroups = gate.shape[0]
    tiles_per_group = rows_per_group // tm
    scratch, sem = _weight_scratch(b, k1 + k2, tn)
    return pl.pallas_call(
        _mm_res_body,
        grid=(n // tn, m // tm),
        in_specs=[pl.BlockSpec((tm, k1), lambda j, i: (i, 0)),
                  pl.BlockSpec((tm, k2), lambda j, i: (i, 0)),
                  pl.BlockSpec((k1 + k2, tn), lambda j, i: (0, j)),
                  pl.BlockSpec((tm, tn), lambda j, i: (i, j)),
                  pl.BlockSpec((1, 1, tn), lambda j, i: (jnp.minimum(i // tiles_per_group, ngroups - 1), 0, j))],
        out_specs=pl.BlockSpec((tm, tn), lambda j, i: (i, j)),
        out_shape=jax.ShapeDtypeStruct((m, n), F32),
        scratch_shapes=scratch,
        compiler_params=_cparams(sem),
        name="matmul_residual",
    )(a1, a2, b, x, gate.reshape(ngroups, 1, n))


def _rope_half(y, cos2, sin2):
    return y * cos2 + pltpu.roll(y, GQA_HEAD_DIM // 2, 1) * sin2


def _rope_quarter(v, cosb, sb1, sb2):
    return v * cosb + pltpu.roll(v, 96, 1) * sb1 + pltpu.roll(v, 32, 1) * sb2


def _rms(x, g):
    return x * lax.rsqrt(jnp.mean(x * x, axis=-1, keepdims=True) + EPS) * g


def _prep_a_body(qa_ref, ka_ref, cq_ref, ckv_ref, kr_ref, cosa_ref, sina_ref, cosb_ref, sb1_ref, sb2_ref,
                 qn_ref, kn_ref, cqg_ref, ckvg_ref, qg_ref, kg_ref, cqn_ref, ckvn_ref, krp_ref):
    cosa, sina = cosa_ref[...], sina_ref[...]
    hd = GQA_HEAD_DIM
    q_scale = hd ** -0.5 * LOG2E
    for h in range(GQA_HEADS):
        y = _rms(qa_ref[:, h * hd:(h + 1) * hd].astype(F32), qn_ref[...])
        qg_ref[:, h * hd:(h + 1) * hd] = (_rope_half(y, cosa, sina) * q_scale).astype(BF16)
    for h in range(GQA_KV_HEADS):
        y = _rms(ka_ref[:, h * hd:(h + 1) * hd].astype(F32), kn_ref[...])
        kg_ref[:, h * hd:(h + 1) * hd] = _rope_half(y, cosa, sina).astype(BF16)
    cqn_ref[...] = _rms(cq_ref[...].astype(F32), cqg_ref[...]).astype(BF16)
    ckvn_ref[...] = _rms(ckv_ref[...].astype(F32), ckvg_ref[...]).astype(BF16)
    krp_ref[...] = _rope_quarter(kr_ref[...].astype(F32), cosb_ref[...], sb1_ref[...], sb2_ref[...]).astype(BF16)


def attn_prep(p, t, tabs, tab_rows, qn_g, kn_g, cq_g, ckv_g):
    r = p.shape[0]
    tm = 256
    tab_tiles = tab_rows // tm
    rowmap = lambda c: (lambda i: (i, c))
    tabmap = lambda i: (i % tab_tiles, 0)
    vec = lambda n: pl.BlockSpec((1, n), lambda i: (0, 0))
    outs = pl.pallas_call(
        _prep_a_body,
        grid=(r // tm,),
        in_specs=[pl.BlockSpec((tm, 2048), rowmap(ATT_QA // 2048)),
                  pl.BlockSpec((tm, 512), rowmap(ATT_KA // 512)),
                  pl.BlockSpec((tm, 1024), rowmap(ATT_CQ // 1024)),
                  pl.BlockSpec((tm, 512), rowmap(ATT_CKV // 512)),
                  pl.BlockSpec((tm, 128), rowmap(ATT_T_KR // 128)),
                  ] + [pl.BlockSpec((tm, 128), tabmap)] * 5 + [vec(128), vec(128), vec(1024), vec(512)],
        out_specs=[pl.BlockSpec((tm, 2048), lambda i: (i, 0)),
                   pl.BlockSpec((tm, 512), lambda i: (i, 0)),
                   pl.BlockSpec((tm, 1024), lambda i: (i, 0)),
                   pl.BlockSpec((tm, 512), lambda i: (i, 0)),
                   pl.BlockSpec((tm, 128), lambda i: (i, 0))],
        out_shape=[jax.ShapeDtypeStruct((r, 2048), BF16), jax.ShapeDtypeStruct((r, 512), BF16),
                   jax.ShapeDtypeStruct((r, 1024), BF16), jax.ShapeDtypeStruct((r, 512), BF16),
                   jax.ShapeDtypeStruct((r, 128), BF16)],
        compiler_params=_cparams(("parallel",)),
        name="attn_prep",
    )(p, p, p, p, t, *tabs, qn_g.reshape(1, -1), kn_g.reshape(1, -1), cq_g.reshape(1, -1), ckv_g.reshape(1, -1))
    return outs


def _mlaq_body(a_ref, b_ref, cosb_ref, sb1_ref, sb2_ref, o_ref):
    acc = jnp.dot(a_ref[...], b_ref[...], preferred_element_type=F32)
    scale = (MLA_NOPE + MLA_ROPE) ** -0.5 * LOG2E
    cosb, sb1, sb2 = cosb_ref[...], sb1_ref[...], sb2_ref[...]
    for h in range(acc.shape[1] // MLA_QK_PAD):
        c0 = h * MLA_QK_PAD
        o_ref[:, c0:c0 + 128] = (acc[:, c0:c0 + 128] * scale).astype(BF16)
        o_ref[:, c0 + 128:c0 + 256] = (_rope_quarter(acc[:, c0 + 128:c0 + 256], cosb, sb1, sb2) * scale).astype(BF16)


def mla_q(cqn, w_uq_pad, tabs_b, tab_rows, tm):
    m, k = cqn.shape
    n = w_uq_pad.shape[1]
    tn = 1024
    tab_tiles = tab_rows // tm
    return pl.pallas_call(
        _mlaq_body,
        grid=(n // tn, m // tm),
        in_specs=[pl.BlockSpec((tm, k), lambda j, i: (i, 0)),
                  pl.BlockSpec((k, tn), lambda j, i: (0, j))] +
                 [pl.BlockSpec((tm, 128), lambda j, i: (i % tab_tiles, 0))] * 3,
        out_specs=pl.BlockSpec((tm, tn), lambda j, i: (i, j)),
        out_shape=jax.ShapeDtypeStruct((m, n), BF16),
        compiler_params=_cparams(("parallel", "parallel")),
        name="mla_q",
    )(cqn, w_uq_pad, *tabs_b)


def _flash_body(*refs, has_lat, kparts, group, tn, tk, s_lat):
    idx = 1
    q_ref = refs[0]
    if has_lat:
        klat = refs[idx:idx + kparts]
        vlat = refs[idx + kparts]
        idx += kparts + 1
    kctx = refs[idx:idx + kparts]
    vctx = refs[idx + kparts]
    idx += kparts + 1
    g_ref, o_ref, m_ref, l_ref, acc_ref, s_ref = refs[idx:idx + 6]

    dv = acc_ref.shape[0]
    ntiles = acc_ref.shape[1] // tn
    dq = q_ref.shape[1] // group
    n_ctx = vctx.shape[0]

    def q_tile(j):
        return q_ref[:, j * dq:(j + 1) * dq] if group > 1 else q_ref[j * tn:(j + 1) * tn, :]

    m_ref[...] = jnp.full(m_ref.shape, -1e30, F32)
    l_ref[...] = jnp.zeros(l_ref.shape, F32)
    acc_ref[...] = jnp.zeros(acc_ref.shape, F32)

    def cat(parts):
        return parts[0] if len(parts) == 1 else jnp.concatenate(parts, axis=1)

    def produce(slot, krefs, off, nk):
        rows = slice(None) if off is None else pl.ds(off, nk)
        kc = cat([r[rows, :] for r in krefs])
        for j in range(ntiles):
            s_ref[slot, :nk, j * tn:(j + 1) * tn] = lax.dot_general(
                kc, q_tile(j), (((1,), (1,)), ((), ())), preferred_element_type=F32)

    def consume(slot, vref, off, nk):
        vc = vref[...] if off is None else vref[pl.ds(off, nk), :]
        for j in range(ntiles):
            sl = slice(j * tn, (j + 1) * tn)
            s = s_ref[slot, :nk, sl]
            m_prev = m_ref[:, sl]
            m_new = jnp.maximum(m_prev, jnp.max(s, axis=0, keepdims=True))
            alpha = jnp.exp2(m_prev - m_new)
            p = jnp.exp2(s - m_new)
            l_ref[:, sl] = alpha * l_ref[:, sl] + jnp.sum(p, axis=0, keepdims=True)
            pv = lax.dot_general(vc, p.astype(BF16), (((0,), (0,)), ((), ())), preferred_element_type=F32)
            acc_ref[:, sl] = alpha * acc_ref[:, sl] + pv
            m_ref[:, sl] = m_new

    if has_lat:
        nchunks = s_lat // tk
        assert nchunks % 2 == 0 and nchunks >= 2
        produce(0, klat, 0, tk)

        def pair(i, carry):
            c0 = pl.multiple_of(2 * i * tk, tk)
            c1 = pl.multiple_of(c0 + tk, tk)
            c2 = pl.multiple_of(c0 + 2 * tk, tk)
            produce(1, klat, c1, tk)
            consume(0, vlat, c0, tk)
            produce(0, klat, c2, tk)
            consume(1, vlat, c1, tk)
            return carry

        lax.fori_loop(0, nchunks // 2 - 1, pair, 0)
        last = (nchunks - 2) * tk
        produce(1, klat, last + tk, tk)
        consume(0, vlat, last, tk)
        produce(0, kctx, None, n_ctx)
        consume(1, vlat, last + tk, tk)
    else:
        produce(0, kctx, None, n_ctx)
    consume(0, vctx, None, n_ctx)

    for j in range(ntiles):
        sl = slice(j * tn, (j + 1) * tn)
        o = jnp.transpose(acc_ref[:, sl] / l_ref[:, sl])
        if group > 1:
            gt = g_ref[:, j * dv:(j + 1) * dv].astype(F32)
            o_ref[:, j * dv:(j + 1) * dv] = (o * _silu(gt)).astype(BF16)
        else:
            gt = g_ref[sl, :].astype(F32)
            o_ref[sl, :] = (o * _silu(gt)).astype(BF16)


def flash(q, klat, vlat, kctx, vctx, gate, *, batch, heads, group, dqk, s_lat, s_ctx, tq, q_is_ctx):
    dv = 128
    kparts = len(kctx)
    has_lat = not q_is_ctx
    nq = (s_ctx if q_is_ctx else s_lat) // tq
    tk = FLASH_TK
    qmap = lambda b, h, i: (b * nq + i, h)
    in_specs = [pl.BlockSpec((tq, group * dqk), qmap)]
    args = [q]
    if has_lat:
        for arr, c0 in klat:
            in_specs.append(pl.BlockSpec((s_lat, 128), functools.partial(lambda b, h, i, c0, per: (b, c0 + per * h), c0=c0[0], per=c0[1])))
            args.append(arr)
        in_specs.append(pl.BlockSpec((s_lat, dv), lambda b, h, i: (b, vlat[1][0] + vlat[1][1] * h)))
        args.append(vlat[0])
    for arr, c0 in kctx:
        in_specs.append(pl.BlockSpec((s_ctx, 128), functools.partial(lambda b, h, i, c0, per: (b, c0 + per * h), c0=c0[0], per=c0[1])))
        args.append(arr)
    in_specs.append(pl.BlockSpec((s_ctx, dv), lambda b, h, i: (b, vctx[1][0] + vctx[1][1] * h)))
    args.append(vctx[0])
    in_specs.append(pl.BlockSpec((tq, group * dv), lambda b, h, i: (b * nq + i, gate[1] + h)))
    args.append(gate[0])
    rows = q.shape[0]
    tn = min(FLASH_TN, tq)
    assert tq % tn == 0 and (group == 1 or tq == tn)
    assert s_ctx <= tk
    body = functools.partial(_flash_body, has_lat=has_lat, kparts=kparts, group=group, tn=tn, tk=tk, s_lat=s_lat)
    return pl.pallas_call(
        body,
        grid=(batch, heads, nq),
        in_specs=in_specs,
        out_specs=pl.BlockSpec((tq, group * dv), qmap),
        out_shape=jax.ShapeDtypeStruct((rows, heads * group * dv), BF16),
        scratch_shapes=[pltpu.VMEM((1, group * tq), F32), pltpu.VMEM((1, group * tq), F32),
                        pltpu.VMEM((dv, group * tq), F32), pltpu.VMEM((2, tk, group * tq), F32)],
        compiler_params=_cparams(("parallel", "parallel", "parallel")),
        name="flash_ctx" if q_is_ctx else "flash_lat",
    )(*args)


def _gla_chunk(q_ref, k_ref, v_ref, gd_ref, wg_ref, bg_ref, st_ref, out_ref, rev):
    c, sub = GLA_CHUNK, GLA_SUB
    nt = (((1,), (1,)), ((), ()))
    v = v_ref[...]
    kf = k_ref[...].astype(F32)
    z = jnp.dot(gd_ref[...].astype(BF16), wg_ref[0], preferred_element_type=F32) + bg_ref[0]
    lg = (jnp.minimum(z, 0.0) - jnp.log(1.0 + jnp.exp(-jnp.abs(z)))) * (LOG2E / GLA_GATE_TAU)
    row = lax.broadcasted_iota(jnp.int32, (c, c), 0)
    col = lax.broadcasted_iota(jnp.int32, (c, c), 1)
    tri = jnp.where((row <= col) if rev else (row >= col), 1.0, 0.0).astype(F32)
    cum = jnp.dot(tri, lg, preferred_element_type=F32, precision=lax.Precision.HIGHEST)
    tot = cum[0:1] if rev else cum[c - 1:c]
    st = st_ref[...]
    yield

    if q_ref is not None:
        qf = q_ref[...].astype(F32) * (GLA_DK ** -0.5)
        lane = col[0:1]

        def level(blk):
            qparts, kparts = [], []
            zero = jnp.zeros((blk, GLA_DK), BF16)
            for a in range(0, c, 2 * blk):
                if rev:
                    i0, j0, cref = a, a + blk, cum[a + blk:a + blk + 1]
                else:
                    i0, j0, cref = a + blk, a, cum[a + blk - 1:a + blk]
                qi = (qf[i0:i0 + blk] * jnp.exp2(cum[i0:i0 + blk] - cref)).astype(BF16)
                kj = (kf[j0:j0 + blk] * jnp.exp2(cref - cum[j0:j0 + blk])).astype(BF16)
                qparts += [qi, zero] if rev else [zero, qi]
                kparts += [zero, kj] if rev else [kj, zero]
            a_l = lax.dot_general(jnp.concatenate(qparts, axis=0), jnp.concatenate(kparts, axis=0), nt,
                                  preferred_element_type=F32)
            sh = (2 * blk).bit_length() - 1
            return jnp.where(jnp.right_shift(row, sh) == jnp.right_shift(col, sh), a_l, 0.0)

        def diag(a):
            qs, ks, cs = qf[a:a + sub], kf[a:a + sub], cum[a:a + sub]
            att = jnp.zeros((sub, c), F32)
            for j in range(sub):
                e = jnp.exp2(cs - cs[j:j + 1])
                colv = jnp.sum(qs * e * ks[j:j + 1], axis=-1, keepdims=True)
                att = jnp.where(lane == a + j, colv, att)
            return att

        blocks = list(range(0, c, sub))
        levels, drows = [], []
        blk = c // 2
        while blk >= sub:
            levels.append(level(blk))
            take = max(1, len(blocks) // 2) if blk > sub else len(blocks)
            drows += [diag(a) for a in blocks[:take]]
            blocks = blocks[take:]
            blk //= 2
        in_order = (col >= row) if rev else (col <= row)
        sh = sub.bit_length() - 1
        same = jnp.right_shift(row, sh) == jnp.right_shift(col, sh)
        att = jnp.where(same & in_order, jnp.concatenate(drows, axis=0), 0.0)
        for a_l in levels:
            att = att + a_l
        yield
        qe = (qf * jnp.exp2(cum)).astype(BF16)
        o = lax.dot_general(qe, st.astype(BF16), nt, preferred_element_type=F32)
        o = o + jnp.dot(att.astype(BF16), v, preferred_element_type=F32)
        out_ref[...] = o.astype(out_ref.dtype)

    kt = (kf * jnp.exp2(tot - cum)).astype(BF16)
    st_ref[...] = st * jnp.exp2(tot) + lax.dot_general(v, kt, (((0,), (0,)), ((), ())), preferred_element_type=F32)


def _gla_body(*refs, with_out, has_init, nsteps):
    per_dir = (4 if with_out else 3) + 2 + (1 if has_init else 0)
    ins = [refs[d * per_dir:(d + 1) * per_dir] for d in range(2)]
    outs = refs[2 * per_dir:2 * per_dir + 2]
    scratch = refs[2 * per_dir + 2:]
    step = pl.program_id(2)

    @pl.when(step == 0)
    def _():
        for d in range(2):
            scratch[d][...] = ins[d][-1][0, 0] if has_init else jnp.zeros(scratch[d].shape, F32)

    chains = []
    for d, rev in enumerate((False, True)):
        r = list(ins[d])
        q_ref = r.pop(0) if with_out else None
        k_ref, v_ref, gd_ref, wg_ref, bg_ref = r[:5]
        chains.append(_gla_chunk(q_ref, k_ref, v_ref, gd_ref, wg_ref, bg_ref, scratch[d], outs[d], rev))
    for _ in itertools.zip_longest(*chains):
        pass

    if not with_out:
        @pl.when(step == nsteps - 1)
        def _():
            for d in range(2):
                outs[d][0, 0] = scratch[d][...]


def gla(q, k, v, gd, wgs, bgs, s0s, *, batch, rows_per_batch):
    c = GLA_CHUNK
    nsteps = rows_per_batch // c
    with_out = q is not None
    has_init = s0s is not None

    def rmap(off, rev, per_head=1):
        if rev:
            return lambda b, h, s: (b * nsteps + nsteps - 1 - s, off + per_head * h)
        return lambda b, h, s: (b * nsteps + s, off + per_head * h)

    in_specs, args = [], []
    for d, rev in enumerate((False, True)):
        if with_out:
            in_specs.append(pl.BlockSpec((c, GLA_DK), rmap(q[1], rev)))
            args.append(q[0])
        in_specs += [pl.BlockSpec((c, GLA_DK), rmap(k[1], rev)),
                     pl.BlockSpec((c, GLA_DV), rmap(v[1], rev)),
                     pl.BlockSpec((c, 128), rmap(gd[1], rev, 0)),
                     pl.BlockSpec((1, 128, GLA_DK), lambda b, h, s: (h, 0, 0)),
                     pl.BlockSpec((1, 1, GLA_DK), lambda b, h, s: (h, 0, 0))]
        args += [k[0], v[0], gd[0], wgs[d], bgs[d]]
        if has_init:
            in_specs.append(pl.BlockSpec((1, 1, GLA_DV, GLA_DK), lambda b, h, s: (b, h, 0, 0)))
            args.append(s0s[d])
    scratch = [pltpu.VMEM((GLA_DV, GLA_DK), F32)] * 2
    if with_out:
        out_specs = [pl.BlockSpec((c, GLA_DV), rmap(0, rev)) for rev in (False, True)]
        out_shape = [jax.ShapeDtypeStruct((batch * rows_per_batch, GLA_HEADS * GLA_DV), BF16)] * 2
    else:
        out_specs = [pl.BlockSpec((1, 1, GLA_DV, GLA_DK), lambda b, h, s: (b, h, 0, 0))] * 2
        out_shape = [jax.ShapeDtypeStruct((batch, GLA_HEADS, GLA_DV, GLA_DK), F32)] * 2
    body = functools.partial(_gla_body, with_out=with_out, has_init=has_init, nsteps=nsteps)
    return pl.pallas_call(
        body,
        grid=(batch, GLA_HEADS, nsteps),
        in_specs=in_specs,
        out_specs=out_specs,
        out_shape=out_shape,
        scratch_shapes=scratch,
        compiler_params=_cparams(("parallel", "parallel", "arbitrary")),
        name="gla_scan" if with_out else "gla_state",
    )(*args)


def _fft_a_body(x_ref, gc_ref, gs_ref, tr_ref, ti_ref, o_ref):
    n1, t2, ch = x_ref.shape
    x = x_ref[...].reshape(n1 * t2, ch)
    yr = jnp.dot(gc_ref[...], x, preferred_element_type=F32)
    yi = -jnp.dot(gs_ref[...], x, preferred_element_type=F32)
    tr, ti = tr_ref[...], ti_ref[...]
    o_ref[:, :, :ch] = (yr * tr - yi * ti).astype(BF16).reshape(n1, t2, ch)
    o_ref[:, :, ch:] = (yr * ti + yi * tr).astype(BF16).reshape(n1, t2, ch)


def _fft_b_body(y_ref, fc_ref, fs_ref, cc_ref, sc_ref, o_ref, *, norm):
    ch = y_ref.shape[1] // 2
    yr, yi = y_ref[:, :ch], y_ref[:, ch:]
    fc, fs = fc_ref[...], fs_ref[...]
    zr = jnp.dot(fc, yr, preferred_element_type=F32) + jnp.dot(fs, yi, preferred_element_type=F32)
    zi = jnp.dot(fc, yi, preferred_element_type=F32) - jnp.dot(fs, yr, preferred_element_type=F32)
    for g in range(FNET_GROUPS):
        sl = slice(g * FNET_DIM, (g + 1) * FNET_DIM)
        f = jnp.dot(zr[:, sl].astype(BF16), cc_ref[...], preferred_element_type=F32)
        f = f + jnp.dot(zi[:, sl].astype(BF16), sc_ref[...], preferred_element_type=F32)
        o_ref[:, sl] = (f * norm).astype(BF16)


def _dft_tables(s):
    n1, n2, t2 = FFT_N1, s // FFT_N1, FFT_T2
    a = np.arange(n1)
    ang1 = 2.0 * np.pi * ((a[:, None] * a[None, :]) % n1) / n1
    eye = np.eye(t2)
    gc = np.kron(np.cos(ang1), eye)
    gs = np.kron(np.sin(ang1), eye)
    b = np.arange(n2)
    angt = 2.0 * np.pi * ((a[:, None] * b[None, :]) % s) / s
    tw = angt.reshape(n1, n2 // t2, t2).transpose(1, 0, 2).reshape(n2 // t2, n1 * t2, 1)
    ang2 = 2.0 * np.pi * ((b[:, None] * b[None, :]) % n2) / n2
    d = np.arange(FNET_DIM)
    angc = 2.0 * np.pi * ((d[:, None] * d[None, :]) % FNET_DIM) / FNET_DIM
    bf = lambda m: jnp.asarray(m, dtype=BF16)
    return dict(gc=bf(gc), gs=bf(gs), tr=jnp.asarray(np.cos(tw), F32), ti=jnp.asarray(-np.sin(tw), F32),
                fc=bf(np.cos(ang2)), fs=bf(np.sin(ang2)), cc=bf(np.cos(angc)), sc=bf(np.sin(angc)))


def fourier_mix(p, u_col, batch, s):
    n1, n2, t2 = FFT_N1, s // FFT_N1, FFT_T2
    ch = FNET_GROUPS * FNET_DIM
    tabs = _dft_tables(s)
    p4 = p.reshape(batch, n1, n2, p.shape[1])
    full = lambda shape: pl.BlockSpec(shape, lambda b, j: (0,) * len(shape))
    y = pl.pallas_call(
        _fft_a_body,
        grid=(batch, n2 // t2),
        in_specs=[pl.BlockSpec((None, n1, t2, ch), lambda b, j: (b, 0, j, u_col // ch)),
                  full((n1 * t2, n1 * t2)), full((n1 * t2, n1 * t2)),
                  pl.BlockSpec((None, n1 * t2, 1), lambda b, j: (j, 0, 0)),
                  pl.BlockSpec((None, n1 * t2, 1), lambda b, j: (j, 0, 0))],
        out_specs=pl.BlockSpec((None, n1, t2, 2 * ch), lambda b, j: (b, 0, j, 0)),
        out_shape=jax.ShapeDtypeStruct((batch, n1, n2, 2 * ch), BF16),
        compiler_params=_cparams(("parallel", "parallel")),
        name="fft_stage_a",
    )(p4, tabs["gc"], tabs["gs"], tabs["tr"], tabs["ti"])
    norm = float(1.0 / np.sqrt(s * FNET_DIM))
    f = pl.pallas_call(
        functools.partial(_fft_b_body, norm=norm),
        grid=(batch, n1),
        in_specs=[pl.BlockSpec((None, None, n2, 2 * ch), lambda b, j: (b, j, 0, 0)),
                  full((n2, n2)), full((n2, n2)), full((FNET_DIM, FNET_DIM)), full((FNET_DIM, FNET_DIM))],
        out_specs=pl.BlockSpec((None, None, n2, ch), lambda b, j: (b, j, 0, 0)),
        out_shape=jax.ShapeDtypeStruct((batch, n1, n2, ch), BF16),
        compiler_params=_cparams(("parallel", "parallel")),
        name="fft_stage_b",
    )(y, tabs["fc"], tabs["fs"], tabs["cc"], tabs["sc"])
    return f.transpose(0, 2, 1, 3).reshape(batch * s, ch)


def _finish_body(of_ref, ob_ref, f_ref, g_ref, ong_ref, o_ref):
    dv = GLA_DV
    for h in range(GLA_HEADS):
        sl = slice(h * dv, (h + 1) * dv)
        o = of_ref[:, sl].astype(F32) + ob_ref[:, sl].astype(F32)
        y = _rms(o, ong_ref[...])
        gt = g_ref[:, sl].astype(F32)
        o_ref[:, sl] = (y * _silu(gt)).astype(BF16)
    base = GLA_HEADS * dv
    gt = g_ref[:, base:].astype(F32)
    o_ref[:, base:] = (f_ref[...].astype(F32) * _silu(gt)).astype(BF16)


def rec_finish(o_f, o_b, f, p, on_g):
    r = o_f.shape[0]
    tm = 256
    w = GLA_HEADS * GLA_DV
    mix = w + FNET_GROUPS * FNET_DIM
    return pl.pallas_call(
        _finish_body,
        grid=(r // tm,),
        in_specs=[pl.BlockSpec((tm, w), lambda i: (i, 0)),
                  pl.BlockSpec((tm, w), lambda i: (i, 0)),
                  pl.BlockSpec((tm, mix - w), lambda i: (i, 0)),
                  pl.BlockSpec((tm, mix), lambda i: (i, REC_T_GATE // mix)),
                  pl.BlockSpec((1, GLA_DV), lambda i: (0, 0))],
        out_specs=pl.BlockSpec((tm, mix), lambda i: (i, 0)),
        out_shape=jax.ShapeDtypeStruct((r, mix), BF16),
        compiler_params=_cparams(("parallel",)),
        name="rec_finish",
    )(o_f, o_b, f, p, on_g.reshape(1, -1))


def _rope_tables(s):
    t = np.arange(s)
    row, col = (t // GRID_W).astype(np.float32), (t % GRID_W).astype(np.float32)

    def cs(rot_dim):
        quarter = rot_dim // 4
        inv_freq = (np.float32(ROPE_THETA) ** (-np.arange(quarter, dtype=np.float32) / quarter)).astype(np.float32)
        ang = np.concatenate([row[:, None] * inv_freq, col[:, None] * inv_freq], axis=-1).astype(np.float32)
        return np.cos(ang), np.sin(ang)

    ca, sa = cs(GQA_HEAD_DIM)
    cb, sb = cs(MLA_ROPE)
    z32, z64 = np.zeros((s, 32), np.float32), np.zeros((s, 64), np.float32)
    tabs = [np.concatenate([ca, ca], 1), np.concatenate([-sa, sa], 1),
            np.concatenate([cb, cb, z64], 1), np.concatenate([-sb, z32, z64], 1), np.concatenate([z32, sb, z64], 1)]
    return [jnp.asarray(t_, F32) for t_ in tabs]


def _identity_rope_tables(n):
    one, zero = np.ones((n, 128), np.float32), np.zeros((n, 128), np.float32)
    cb = np.concatenate([np.ones((n, 64), np.float32), np.zeros((n, 64), np.float32)], 1)
    return [jnp.asarray(t_, F32) for t_ in (one, zero, cb, zero, zero)]


def attention_calls(p_lat, p_ctx, t_lat, t_ctx, qg_l, qg_c, kg_l, kg_c, qm_l, qm_c, kv_l, kv_c, krp_l, krp_c,
                    batch, s, n_ctx):
    va = (ATT_VA // 128, 1)
    gate_a_col, gate_b_col = ATT_T_GATE // 512, (ATT_T_GATE + GQA_HEADS * GQA_HEAD_DIM) // 128
    common = dict(batch=batch, s_lat=s, s_ctx=n_ctx)
    gqa = dict(heads=GQA_KV_HEADS, group=GQA_GROUP, dqk=GQA_HEAD_DIM, **common)
    mla = dict(heads=MLA_HEADS, group=1, dqk=MLA_QK_PAD, **common)
    kb_l, kb_c = [(kv_l, (0, 2)), (krp_l, (0, 0))], [(kv_c, (0, 2)), (krp_c, (0, 0))]
    ya_l = flash(qg_l, [(kg_l, (0, 1))], (p_lat, va), [(kg_c, (0, 1))], (p_ctx, va), (t_lat, gate_a_col),
                 tq=FLASH_TN, q_is_ctx=False, **gqa)
    yb_l = flash(qm_l, kb_l, (kv_l, (1, 2)), kb_c, (kv_c, (1, 2)), (t_lat, gate_b_col),
                 tq=2 * FLASH_TN, q_is_ctx=False, **mla)
    ya_c = flash(qg_c, None, None, [(kg_c, (0, 1))], (p_ctx, va), (t_ctx, gate_a_col),
                 tq=n_ctx, q_is_ctx=True, **gqa)
    yb_c = flash(qm_c, None, None, kb_c, (kv_c, (1, 2)), (t_ctx, gate_b_col), tq=n_ctx, q_is_ctx=True, **mla)
    return ya_l, ya_c, yb_l, yb_c


def _attention_layer(x_lat, x_ctx, mod, norm_g, w_in, qn_g, kn_g, cq_g, ckv_g, w_uq, w_ukv, w_out, batch, s, n_ctx):
    d = x_lat.shape[1]
    shift, scale, gate = mod[:, :d], mod[:, d:2 * d], mod[:, 2 * d:]
    h_lat = modnorm(x_lat, norm_g, scale[:batch], shift[:batch], s, BF16)
    h_ctx = modnorm(x_ctx, norm_g, scale[batch:batch + 1], shift[batch:batch + 1], batch * n_ctx, BF16)

    kr_end = ATT_KR + MLA_ROPE
    w_tail = jnp.concatenate([w_in[:, kr_end:], w_in[:, ATT_KR:kr_end],
                              jnp.zeros((d, ATT_TAIL - (w_in.shape[1] - ATT_KR)), w_in.dtype)], axis=1).astype(BF16)
    w_uq_p = jnp.pad(w_uq.reshape(MLA_Q_LORA, MLA_HEADS, MLA_NOPE + MLA_ROPE),
                     ((0, 0), (0, 0), (0, MLA_QK_PAD - MLA_NOPE - MLA_ROPE))).reshape(MLA_Q_LORA, MLA_HEADS * MLA_QK_PAD).astype(BF16)

    ctx_rows = batch * n_ctx
    p_lat = matmul(h_lat, w_in, 1024, 512, BF16, n=ATT_MAIN)
    p_ctx = matmul(h_ctx, w_in, ctx_rows, 512, BF16, n=ATT_MAIN)
    t_lat = matmul(h_lat, w_tail, 1024, 512, BF16)
    t_ctx = matmul(h_ctx, w_tail, ctx_rows, 512, BF16)

    tabs_lat, tabs_ctx = _rope_tables(s), _identity_rope_tables(ctx_rows)
    qg_l, kg_l, cqn_l, ckvn_l, krp_l = attn_prep(p_lat, t_lat, tabs_lat, s, qn_g, kn_g, cq_g, ckv_g)
    qg_c, kg_c, cqn_c, ckvn_c, krp_c = attn_prep(p_ctx, t_ctx, tabs_ctx, ctx_rows, qn_g, kn_g, cq_g, ckv_g)
    qm_l = mla_q(cqn_l, w_uq_p, tabs_lat[2:], s, 1024)
    qm_c = mla_q(cqn_c, w_uq_p, tabs_ctx[2:], ctx_rows, ctx_rows)
    kv_l = matmul(ckvn_l, w_ukv, 1024, 1024, BF16)
    kv_c = matmul(ckvn_c, w_ukv, ctx_rows, 1024, BF16)

    ya_l, ya_c, yb_l, yb_c = attention_calls(p_lat, p_ctx, t_lat, t_ctx, qg_l, qg_c, kg_l, kg_c, qm_l, qm_c,
                                             kv_l, kv_c, krp_l, krp_c, batch, s, n_ctx)

    x_lat = matmul_residual(ya_l, yb_l, w_out, x_lat, gate[:batch], s, 1024, 512)
    x_ctx = matmul_residual(ya_c, yb_c, w_out, x_ctx, gate[batch:batch + 1], ctx_rows, ctx_rows, 512)
    return x_lat, x_ctx


def _recurrent_layer(x_lat, x_ctx, mod, norm_g, w_in, wg_f, bg_f, wg_b, bg_b, on_g, w_out, batch, s, n_ctx):
    d = x_lat.shape[1]
    shift, scale, gate = mod[:, :d], mod[:, d:2 * d], mod[:, 2 * d:]
    h_lat = modnorm(x_lat, norm_g, scale[:batch], shift[:batch], s, BF16)
    h_ctx = modnorm(x_ctx, norm_g, scale[batch:batch + 1], shift[batch:batch + 1], batch * n_ctx, BF16)

    gd_end = REC_MAIN + 2 * GLA_GATE_RANK
    u_end = gd_end + FNET_GROUPS * FNET_DIM
    w_tail = jnp.concatenate([w_in[:, u_end:], w_in[:, gd_end:u_end]], axis=1).astype(BF16)
    w_gd = jnp.pad(w_in[:, REC_MAIN:gd_end], ((0, 0), (0, 128 - 2 * GLA_GATE_RANK))).astype(BF16)

    def gate_w(wg, row0):
        w3 = wg.reshape(GLA_GATE_RANK, GLA_HEADS, GLA_DK).transpose(1, 0, 2)
        return jnp.pad(w3, ((0, 0), (row0, 128 - GLA_GATE_RANK - row0), (0, 0))).astype(BF16)

    wgf_p, wgb_p = gate_w(wg_f, 0), gate_w(wg_b, GLA_GATE_RANK)
    bgf, bgb = bg_f.reshape(GLA_HEADS, 1, GLA_DK), bg_b.reshape(GLA_HEADS, 1, GLA_DK)

    ctx_rows = batch * n_ctx
    p_lat = matmul(h_lat, w_in, 1024, 512, BF16, n=REC_MAIN)
    p_ctx = matmul(h_ctx, w_in, ctx_rows, 512, BF16, n=REC_MAIN)
    t_lat = matmul(h_lat, w_tail, 1024, 1024, BF16)
    gd_lat = matmul(h_lat, w_gd, 1024, 128, F32)
    gd_ctx = matmul(h_ctx, w_gd, ctx_rows, 128, F32)

    kq, kk, kv = REC_Q // GLA_DK, REC_K // GLA_DK, REC_V // GLA_DV
    wgs, bgs = (wgf_p, wgb_p), (bgf, bgb)
    states = gla(None, (p_ctx, kk), (p_ctx, kv), (gd_ctx, 0), wgs, bgs, None, batch=batch, rows_per_batch=n_ctx)
    o_f, o_b = gla((p_lat, kq), (p_lat, kk), (p_lat, kv), (gd_lat, 0), wgs, bgs, states, batch=batch, rows_per_batch=s)

    f = fourier_mix(t_lat, REC_T_U, batch, s)
    y = rec_finish(o_f, o_b, f, t_lat, on_g)
    return matmul_residual_single(y, w_out, x_lat, gate[:batch], s, 1024, 512)


def _mm_res1_body(a_ref, b_ref, x_ref, g_ref, o_ref, *wb):
    acc = jnp.dot(a_ref[...], _resident_weight(b_ref, wb)[...], preferred_element_type=F32)
    o_ref[...] = x_ref[...] + g_ref[0] * acc


def matmul_residual_single(a, b, x, gate, rows_per_group, tm, tn):
    m, k = a.shape
    n = b.shape[1]
    ngroups = gate.shape[0]
    tiles_per_group = rows_per_group // tm
    scratch, sem = _weight_scratch(b, k, tn)
    return pl.pallas_call(
        _mm_res1_body,
        grid=(n // tn, m // tm),
        in_specs=[pl.BlockSpec((tm, k), lambda j, i: (i, 0)),
                  pl.BlockSpec((k, tn), lambda j, i: (0, j)),
                  pl.BlockSpec((tm, tn), lambda j, i: (i, j)),
                  pl.BlockSpec((1, 1, tn), lambda j, i: (jnp.minimum(i // tiles_per_group, ngroups - 1), 0, j))],
        out_specs=pl.BlockSpec((tm, tn), lambda j, i: (i, j)),
        out_shape=jax.ShapeDtypeStruct((m, n), F32),
        scratch_shapes=scratch,
        compiler_params=_cparams(sem),
        name="matmul_residual1",
    )(a, b, x, gate.reshape(ngroups, 1, n))


def kernel(x, c, ctx, c_ctx, norm_g, ada_w, ada_b, att_w_in, att_qn_g, att_kn_g, mla_cq_g, mla_ckv_g, mla_w_uq,
           mla_w_ukv, att_w_out, rec_w_in, gla_wg_f, gla_bg_f, gla_wg_b, gla_bg_b, gla_on_g, rec_w_out, final_g):
    batch, s, d = x.shape
    n_ctx = ctx.shape[1]
    x_lat = x.reshape(batch * s, d)
    x_ctx = ctx.reshape(batch * n_ctx, d)
    cond = jnp.concatenate([c, c_ctx[None, :], jnp.zeros((8 - batch - 1, d), F32)], axis=0)
    mods = adaln(cond, ada_w, ada_b)

    x_lat, x_ctx = _attention_layer(x_lat, x_ctx, mods[0], norm_g[0], att_w_in[0], att_qn_g[0], att_kn_g[0],
                                    mla_cq_g[0], mla_ckv_g[0], mla_w_uq[0], mla_w_ukv[0], att_w_out[0],
                                    batch, s, n_ctx)
    x_lat = _recurrent_layer(x_lat, x_ctx, mods[1], norm_g[1], rec_w_in[0], gla_wg_f[0], gla_bg_f[0], gla_wg_b[0],
                             gla_bg_b[0], gla_on_g[0], rec_w_out[0], batch, s, n_ctx)
    zero = jnp.zeros((1, d), F32)
    out = modnorm(x_lat, final_g, zero, zero, batch * s, F32)
    return out.reshape(batch, s, d)
```

```python
import functools
import itertools

import numpy as np
import jax
import jax.numpy as jnp
from jax import lax
from jax.experimental import pallas as pl
from jax.experimental.pallas import tpu as pltpu

F32 = jnp.float32
BF16 = jnp.bfloat16

EPS = 1e-6
GRID_W = 64
ROPE_THETA = 10000.0

GQA_HEADS = 16
GQA_KV_HEADS = 4
GQA_GROUP = GQA_HEADS // GQA_KV_HEADS
GQA_HEAD_DIM = 128
MLA_HEADS = 16
MLA_Q_LORA = 1024
MLA_KV_LORA = 512
MLA_NOPE = 128
MLA_ROPE = 64
MLA_V = 128
MLA_QK_PAD = 256

GLA_HEADS = 6
GLA_DK = 256
GLA_DV = 512
GLA_GATE_RANK = 16
GLA_GATE_TAU = 16.0
GLA_CHUNK = 128
GLA_SUB = 8
FNET_GROUPS = 4
FNET_DIM = 256
FLASH_TN = 512
FLASH_TK = 1024
LOG2E = 1.4426950408889634
FFT_N1 = 64
FFT_T2 = 16

ATT_QA, ATT_KA, ATT_VA, ATT_CQ, ATT_CKV, ATT_KR, ATT_GATE, ATT_IN_PAD = 0, 2048, 2560, 3072, 4096, 4608, 5120, 9216
REC_GATE, REC_Q, REC_K, REC_V, REC_U, REC_IN_MAIN = 0, 4096, 5632, 7168, 10240, 11264

VMEM_LIMIT = 56 * 1024 * 1024


def _cparams(sem):
    return pltpu.CompilerParams(dimension_semantics=sem, vmem_limit_bytes=VMEM_LIMIT)


def _silu(x):
    return x * jax.nn.sigmoid(x)


def _adaln_body(a_ref, w_ref, b_ref, o_ref):
    a = _silu(a_ref[...]).astype(BF16)
    o_ref[0] = jnp.dot(a, w_ref[0].astype(BF16), preferred_element_type=F32) + b_ref[0]


def adaln(cond, ada_w, ada_b):
    depth, d, n = ada_w.shape
    tn = 512
    return pl.pallas_call(
        _adaln_body,
        grid=(depth, n // tn),
        in_specs=[pl.BlockSpec((8, d), lambda l, j: (0, 0)),
                  pl.BlockSpec((1, d, tn), lambda l, j: (l, 0, j)),
                  pl.BlockSpec((1, 1, tn), lambda l, j: (l, 0, j))],
        out_specs=pl.BlockSpec((1, 8, tn), lambda l, j: (l, 0, j)),
        out_shape=jax.ShapeDtypeStruct((depth, 8, n), F32),
        compiler_params=_cparams(("parallel", "parallel")),
        name="adaln",
    )(cond, ada_w, ada_b.reshape(depth, 1, n))


def _modnorm_body(x_ref, g_ref, sc_ref, sh_ref, o_ref):
    x = x_ref[...]
    y = x * lax.rsqrt(jnp.mean(x * x, axis=-1, keepdims=True) + EPS)
    y = y * g_ref[...]
    o_ref[...] = (y * (1.0 + sc_ref[0]) + sh_ref[0]).astype(o_ref.dtype)


def modnorm(x, g, scale, shift, rows_per_group, out_dtype):
    r, d = x.shape
    tm = 256
    ngroups = scale.shape[0]
    tiles_per_group = rows_per_group // tm
    gmap = lambda i: (jnp.minimum(i // tiles_per_group, ngroups - 1), 0, 0)
    return pl.pallas_call(
        _modnorm_body,
        grid=(r // tm,),
        in_specs=[pl.BlockSpec((tm, d), lambda i: (i, 0)),
                  pl.BlockSpec((1, d), lambda i: (0, 0)),
                  pl.BlockSpec((1, 1, d), gmap),
                  pl.BlockSpec((1, 1, d), gmap)],
        out_specs=pl.BlockSpec((tm, d), lambda i: (i, 0)),
        out_shape=jax.ShapeDtypeStruct((r, d), out_dtype),
        compiler_params=_cparams(("parallel",)),
        name="modnorm",
    )(x, g.reshape(1, d), scale.reshape(ngroups, 1, d), shift.reshape(ngroups, 1, d))


def _mm_body(a_ref, b_ref, o_ref):
    o_ref[...] = jnp.dot(a_ref[...], b_ref[...], preferred_element_type=F32).astype(o_ref.dtype)


def matmul(a, b, tm, tn, out_dtype):
    m, k = a.shape
    n = b.shape[1]
    return pl.pallas_call(
        _mm_body,
        grid=(n // tn, m // tm),
        in_specs=[pl.BlockSpec((tm, k), lambda j, i: (i, 0)),
                  pl.BlockSpec((k, tn), lambda j, i: (0, j))],
        out_specs=pl.BlockSpec((tm, tn), lambda j, i: (i, j)),
        out_shape=jax.ShapeDtypeStruct((m, n), out_dtype),
        compiler_params=_cparams(("parallel", "parallel")),
        name="matmul",
    )(a, b)


def _mm_res_body(a1_ref, a2_ref, b_ref, x_ref, g_ref, o_ref):
    k1 = a1_ref.shape[1]
    acc = jnp.dot(a1_ref[...], b_ref[:k1, :], preferred_element_type=F32)
    acc = acc + jnp.dot(a2_ref[...], b_ref[k1:, :], preferred_element_type=F32)
    o_ref[...] = x_ref[...] + g_ref[0] * acc


def matmul_residual(a1, a2, b, x, gate, rows_per_group, tm, tn):
    m, k1 = a1.shape
    k2 = a2.shape[1]
    n = b.shape[1]
    ngroups = gate.shape[0]
    tiles_per_group = rows_per_group // tm
    return pl.pallas_call(
        _mm_res_body,
        grid=(n // tn, m // tm),
        in_specs=[pl.BlockSpec((tm, k1), lambda j, i: (i, 0)),
                  pl.BlockSpec((tm, k2), lambda j, i: (i, 0)),
                  pl.BlockSpec((k1 + k2, tn), lambda j, i: (0, j)),
                  pl.BlockSpec((tm, tn), lambda j, i: (i, j)),
                  pl.BlockSpec((1, 1, tn), lambda j, i: (jnp.minimum(i // tiles_per_group, ngroups - 1), 0, j))],
        out_specs=pl.BlockSpec((tm, tn), lambda j, i: (i, j)),
        out_shape=jax.ShapeDtypeStruct((m, n), F32),
        compiler_params=_cparams(("parallel", "parallel")),
        name="matmul_residual",
    )(a1, a2, b, x, gate.reshape(ngroups, 1, n))


def _rope_half(y, cos2, sin2):
    return y * cos2 + pltpu.roll(y, GQA_HEAD_DIM // 2, 1) * sin2


def _rope_quarter(v, cosb, sb1, sb2):
    return v * cosb + pltpu.roll(v, 96, 1) * sb1 + pltpu.roll(v, 32, 1) * sb2


def _rms(x, g):
    return x * lax.rsqrt(jnp.mean(x * x, axis=-1, keepdims=True) + EPS) * g


def _prep_a_body(qa_ref, ka_ref, cq_ref, ckv_ref, kr_ref, cosa_ref, sina_ref, cosb_ref, sb1_ref, sb2_ref,
                 qn_ref, kn_ref, cqg_ref, ckvg_ref, qg_ref, kg_ref, cqn_ref, ckvn_ref, krp_ref):
    cosa, sina = cosa_ref[...], sina_ref[...]
    hd = GQA_HEAD_DIM
    q_scale = hd ** -0.5 * LOG2E
    for h in range(GQA_HEADS):
        y = _rms(qa_ref[:, h * hd:(h + 1) * hd].astype(F32), qn_ref[...])
        qg_ref[:, h * hd:(h + 1) * hd] = (_rope_half(y, cosa, sina) * q_scale).astype(BF16)
    for h in range(GQA_KV_HEADS):
        y = _rms(ka_ref[:, h * hd:(h + 1) * hd].astype(F32), kn_ref[...])
        kg_ref[:, h * hd:(h + 1) * hd] = _rope_half(y, cosa, sina).astype(BF16)
    cqn_ref[...] = _rms(cq_ref[...].astype(F32), cqg_ref[...]).astype(BF16)
    ckvn_ref[...] = _rms(ckv_ref[...].astype(F32), ckvg_ref[...]).astype(BF16)
    krp_ref[...] = _rope_quarter(kr_ref[...].astype(F32), cosb_ref[...], sb1_ref[...], sb2_ref[...]).astype(BF16)


def attn_prep(p, tabs, tab_rows, qn_g, kn_g, cq_g, ckv_g):
    r = p.shape[0]
    tm = 256
    tab_tiles = tab_rows // tm
    rowmap = lambda c: (lambda i: (i, c))
    tabmap = lambda i: (i % tab_tiles, 0)
    vec = lambda n: pl.BlockSpec((1, n), lambda i: (0, 0))
    outs = pl.pallas_call(
        _prep_a_body,
        grid=(r // tm,),
        in_specs=[pl.BlockSpec((tm, 2048), rowmap(ATT_QA // 2048)),
                  pl.BlockSpec((tm, 512), rowmap(ATT_KA // 512)),
                  pl.BlockSpec((tm, 1024), rowmap(ATT_CQ // 1024)),
                  pl.BlockSpec((tm, 512), rowmap(ATT_CKV // 512)),
                  pl.BlockSpec((tm, 128), rowmap(ATT_KR // 128)),
                  ] + [pl.BlockSpec((tm, 128), tabmap)] * 5 + [vec(128), vec(128), vec(1024), vec(512)],
        out_specs=[pl.BlockSpec((tm, 2048), lambda i: (i, 0)),
                   pl.BlockSpec((tm, 512), lambda i: (i, 0)),
                   pl.BlockSpec((tm, 1024), lambda i: (i, 0)),
                   pl.BlockSpec((tm, 512), lambda i: (i, 0)),
                   pl.BlockSpec((tm, 128), lambda i: (i, 0))],
        out_shape=[jax.ShapeDtypeStruct((r, 2048), BF16), jax.ShapeDtypeStruct((r, 512), BF16),
                   jax.ShapeDtypeStruct((r, 1024), BF16), jax.ShapeDtypeStruct((r, 512), BF16),
                   jax.ShapeDtypeStruct((r, 128), BF16)],
        compiler_params=_cparams(("parallel",)),
        name="attn_prep",
    )(p, p, p, p, p, *tabs, qn_g.reshape(1, -1), kn_g.reshape(1, -1), cq_g.reshape(1, -1), ckv_g.reshape(1, -1))
    return outs


def _mlaq_body(a_ref, b_ref, cosb_ref, sb1_ref, sb2_ref, o_ref):
    acc = jnp.dot(a_ref[...], b_ref[...], preferred_element_type=F32)
    scale = (MLA_NOPE + MLA_ROPE) ** -0.5 * LOG2E
    cosb, sb1, sb2 = cosb_ref[...], sb1_ref[...], sb2_ref[...]
    for h in range(acc.shape[1] // MLA_QK_PAD):
        c0 = h * MLA_QK_PAD
        o_ref[:, c0:c0 + 128] = (acc[:, c0:c0 + 128] * scale).astype(BF16)
        o_ref[:, c0 + 128:c0 + 256] = (_rope_quarter(acc[:, c0 + 128:c0 + 256], cosb, sb1, sb2) * scale).astype(BF16)


def mla_q(cqn, w_uq_pad, tabs_b, tab_rows, tm):
    m, k = cqn.shape
    n = w_uq_pad.shape[1]
    tn = 1024
    tab_tiles = tab_rows // tm
    return pl.pallas_call(
        _mlaq_body,
        grid=(n // tn, m // tm),
        in_specs=[pl.BlockSpec((tm, k), lambda j, i: (i, 0)),
                  pl.BlockSpec((k, tn), lambda j, i: (0, j))] +
                 [pl.BlockSpec((tm, 128), lambda j, i: (i % tab_tiles, 0))] * 3,
        out_specs=pl.BlockSpec((tm, tn), lambda j, i: (i, j)),
        out_shape=jax.ShapeDtypeStruct((m, n), BF16),
        compiler_params=_cparams(("parallel", "parallel")),
        name="mla_q",
    )(cqn, w_uq_pad, *tabs_b)


def _flash_body(*refs, has_lat, kparts, group, tn, tk, s_lat):
    idx = 1
    q_ref = refs[0]
    if has_lat:
        klat = refs[idx:idx + kparts]
        vlat = refs[idx + kparts]
        idx += kparts + 1
    kctx = refs[idx:idx + kparts]
    vctx = refs[idx + kparts]
    idx += kparts + 1
    g_ref, o_ref, m_ref, l_ref, acc_ref, s_ref = refs[idx:idx + 6]

    dv = acc_ref.shape[0]
    ntiles = acc_ref.shape[1] // tn
    dq = q_ref.shape[1] // group
    n_ctx = vctx.shape[0]

    def q_tile(j):
        return q_ref[:, j * dq:(j + 1) * dq] if group > 1 else q_ref[j * tn:(j + 1) * tn, :]

    m_ref[...] = jnp.full(m_ref.shape, -1e30, F32)
    l_ref[...] = jnp.zeros(l_ref.shape, F32)
    acc_ref[...] = jnp.zeros(acc_ref.shape, F32)

    def cat(parts):
        return parts[0] if len(parts) == 1 else jnp.concatenate(parts, axis=1)

    def produce(slot, krefs, off, nk):
        rows = slice(None) if off is None else pl.ds(off, nk)
        kc = cat([r[rows, :] for r in krefs])
        for j in range(ntiles):
            s_ref[slot, :nk, j * tn:(j + 1) * tn] = lax.dot_general(
                kc, q_tile(j), (((1,), (1,)), ((), ())), preferred_element_type=F32)

    def consume(slot, vref, off, nk):
        vc = vref[...] if off is None else vref[pl.ds(off, nk), :]
        for j in range(ntiles):
            sl = slice(j * tn, (j + 1) * tn)
            s = s_ref[slot, :nk, sl]
            m_prev = m_ref[:, sl]
            m_new = jnp.maximum(m_prev, jnp.max(s, axis=0, keepdims=True))
            alpha = jnp.exp2(m_prev - m_new)
            p = jnp.exp2(s - m_new)
            l_ref[:, sl] = alpha * l_ref[:, sl] + jnp.sum(p, axis=0, keepdims=True)
            pv = lax.dot_general(vc, p.astype(BF16), (((0,), (0,)), ((), ())), preferred_element_type=F32)
            acc_ref[:, sl] = alpha * acc_ref[:, sl] + pv
            m_ref[:, sl] = m_new

    if has_lat:
        nchunks = s_lat // tk
        assert nchunks % 2 == 0 and nchunks >= 2
        produce(0, klat, 0, tk)

        def pair(i, carry):
            c0 = pl.multiple_of(2 * i * tk, tk)
            c1 = pl.multiple_of(c0 + tk, tk)
            c2 = pl.multiple_of(c0 + 2 * tk, tk)
            produce(1, klat, c1, tk)
            consume(0, vlat, c0, tk)
            produce(0, klat, c2, tk)
            consume(1, vlat, c1, tk)
            return carry

        lax.fori_loop(0, nchunks // 2 - 1, pair, 0)
        last = (nchunks - 2) * tk
        produce(1, klat, last + tk, tk)
        consume(0, vlat, last, tk)
        produce(0, kctx, None, n_ctx)
        consume(1, vlat, last + tk, tk)
    else:
        produce(0, kctx, None, n_ctx)
    consume(0, vctx, None, n_ctx)

    for j in range(ntiles):
        sl = slice(j * tn, (j + 1) * tn)
        o = jnp.transpose(acc_ref[:, sl] / l_ref[:, sl])
        if group > 1:
            gt = g_ref[:, j * dv:(j + 1) * dv].astype(F32)
            o_ref[:, j * dv:(j + 1) * dv] = (o * _silu(gt)).astype(BF16)
        else:
            gt = g_ref[sl, :].astype(F32)
            o_ref[sl, :] = (o * _silu(gt)).astype(BF16)


def flash(q, klat, vlat, kctx, vctx, gate, *, batch, heads, group, dqk, s_lat, s_ctx, tq, q_is_ctx):
    dv = 128
    kparts = len(kctx)
    has_lat = not q_is_ctx
    nq = (s_ctx if q_is_ctx else s_lat) // tq
    tk = FLASH_TK
    qmap = lambda b, h, i: (b * nq + i, h)
    in_specs = [pl.BlockSpec((tq, group * dqk), qmap)]
    args = [q]
    if has_lat:
        for arr, c0 in klat:
            in_specs.append(pl.BlockSpec((s_lat, 128), functools.partial(lambda b, h, i, c0, per: (b, c0 + per * h), c0=c0[0], per=c0[1])))
            args.append(arr)
        in_specs.append(pl.BlockSpec((s_lat, dv), lambda b, h, i: (b, vlat[1] + h)))
        args.append(vlat[0])
    for arr, c0 in kctx:
        in_specs.append(pl.BlockSpec((s_ctx, 128), functools.partial(lambda b, h, i, c0, per: (b, c0 + per * h), c0=c0[0], per=c0[1])))
        args.append(arr)
    in_specs.append(pl.BlockSpec((s_ctx, dv), lambda b, h, i: (b, vctx[1] + h)))
    args.append(vctx[0])
    in_specs.append(pl.BlockSpec((tq, group * dv), lambda b, h, i: (b * nq + i, gate[1] + h)))
    args.append(gate[0])
    rows = q.shape[0]
    tn = min(FLASH_TN, tq)
    assert tq % tn == 0 and (group == 1 or tq == tn)
    assert s_ctx <= tk
    body = functools.partial(_flash_body, has_lat=has_lat, kparts=kparts, group=group, tn=tn, tk=tk, s_lat=s_lat)
    return pl.pallas_call(
        body,
        grid=(batch, heads, nq),
        in_specs=in_specs,
        out_specs=pl.BlockSpec((tq, group * dv), qmap),
        out_shape=jax.ShapeDtypeStruct((rows, heads * group * dv), BF16),
        scratch_shapes=[pltpu.VMEM((1, group * tq), F32), pltpu.VMEM((1, group * tq), F32),
                        pltpu.VMEM((dv, group * tq), F32), pltpu.VMEM((2, tk, group * tq), F32)],
        compiler_params=_cparams(("parallel", "parallel", "parallel")),
        name="flash_ctx" if q_is_ctx else "flash_lat",
    )(*args)


def _gla_chunk(q_ref, k_ref, v_ref, gd_ref, wg_ref, bg_ref, st_ref, out_ref, rev):
    c, sub = GLA_CHUNK, GLA_SUB
    nt = (((1,), (1,)), ((), ()))
    v = v_ref[...]
    kf = k_ref[...].astype(F32)
    z = jnp.dot(gd_ref[...].astype(BF16), wg_ref[0], preferred_element_type=F32) + bg_ref[0]
    lg = (jnp.minimum(z, 0.0) - jnp.log(1.0 + jnp.exp(-jnp.abs(z)))) * (LOG2E / GLA_GATE_TAU)
    row = lax.broadcasted_iota(jnp.int32, (c, c), 0)
    col = lax.broadcasted_iota(jnp.int32, (c, c), 1)
    tri = jnp.where((row <= col) if rev else (row >= col), 1.0, 0.0).astype(F32)
    cum = jnp.dot(tri, lg, preferred_element_type=F32, precision=lax.Precision.HIGHEST)
    tot = cum[0:1] if rev else cum[c - 1:c]
    st = st_ref[...]
    yield

    if q_ref is not None:
        qf = q_ref[...].astype(F32) * (GLA_DK ** -0.5)
        lane = col[0:1]

        def level(blk):
            qparts, kparts = [], []
            zero = jnp.zeros((blk, GLA_DK), BF16)
            for a in range(0, c, 2 * blk):
                if rev:
                    i0, j0, cref = a, a + blk, cum[a + blk:a + blk + 1]
                else:
                    i0, j0, cref = a + blk, a, cum[a + blk - 1:a + blk]
                qi = (qf[i0:i0 + blk] * jnp.exp2(cum[i0:i0 + blk] - cref)).astype(BF16)
                kj = (kf[j0:j0 + blk] * jnp.exp2(cref - cum[j0:j0 + blk])).astype(BF16)
                qparts += [qi, zero] if rev else [zero, qi]
                kparts += [zero, kj] if rev else [kj, zero]
            a_l = lax.dot_general(jnp.concatenate(qparts, axis=0), jnp.concatenate(kparts, axis=0), nt,
                                  preferred_element_type=F32)
            sh = (2 * blk).bit_length() - 1
            return jnp.where(jnp.right_shift(row, sh) == jnp.right_shift(col, sh), a_l, 0.0)

        def diag(a):
            qs, ks, cs = qf[a:a + sub], kf[a:a + sub], cum[a:a + sub]
            att = jnp.zeros((sub, c), F32)
            for j in range(sub):
                e = jnp.exp2(cs - cs[j:j + 1])
                colv = jnp.sum(qs * e * ks[j:j + 1], axis=-1, keepdims=True)
                att = jnp.where(lane == a + j, colv, att)
            return att

        blocks = list(range(0, c, sub))
        levels, drows = [], []
        blk = c // 2
        while blk >= sub:
            levels.append(level(blk))
            take = max(1, len(blocks) // 2) if blk > sub else len(blocks)
            drows += [diag(a) for a in blocks[:take]]
            blocks = blocks[take:]
            blk //= 2
        in_order = (col >= row) if rev else (col <= row)
        sh = sub.bit_length() - 1
        same = jnp.right_shift(row, sh) == jnp.right_shift(col, sh)
        att = jnp.where(same & in_order, jnp.concatenate(drows, axis=0), 0.0)
        for a_l in levels:
            att = att + a_l
        yield
        qe = (qf * jnp.exp2(cum)).astype(BF16)
        o = lax.dot_general(qe, st.astype(BF16), nt, preferred_element_type=F32)
        o = o + jnp.dot(att.astype(BF16), v, preferred_element_type=F32)
        out_ref[...] = o.astype(out_ref.dtype)

    kt = (kf * jnp.exp2(tot - cum)).astype(BF16)
    st_ref[...] = st * jnp.exp2(tot) + lax.dot_general(v, kt, (((0,), (0,)), ((), ())), preferred_element_type=F32)


def _gla_body(*refs, with_out, has_init, nsteps):
    per_dir = (4 if with_out else 3) + 2 + (1 if has_init else 0)
    ins = [refs[d * per_dir:(d + 1) * per_dir] for d in range(2)]
    outs = refs[2 * per_dir:2 * per_dir + 2]
    scratch = refs[2 * per_dir + 2:]
    step = pl.program_id(2)

    @pl.when(step == 0)
    def _():
        for d in range(2):
            scratch[d][...] = ins[d][-1][0, 0] if has_init else jnp.zeros(scratch[d].shape, F32)

    chains = []
    for d, rev in enumerate((False, True)):
        r = list(ins[d])
        q_ref = r.pop(0) if with_out else None
        k_ref, v_ref, gd_ref, wg_ref, bg_ref = r[:5]
        chains.append(_gla_chunk(q_ref, k_ref, v_ref, gd_ref, wg_ref, bg_ref, scratch[d], outs[d], rev))
    for _ in itertools.zip_longest(*chains):
        pass

    if not with_out:
        @pl.when(step == nsteps - 1)
        def _():
            for d in range(2):
                outs[d][0, 0] = scratch[d][...]


def gla(q, k, v, gd, wgs, bgs, s0s, *, batch, rows_per_batch):
    c = GLA_CHUNK
    nsteps = rows_per_batch // c
    with_out = q is not None
    has_init = s0s is not None

    def rmap(off, rev, per_head=1):
        if rev:
            return lambda b, h, s: (b * nsteps + nsteps - 1 - s, off + per_head * h)
        return lambda b, h, s: (b * nsteps + s, off + per_head * h)

    in_specs, args = [], []
    for d, rev in enumerate((False, True)):
        if with_out:
            in_specs.append(pl.BlockSpec((c, GLA_DK), rmap(q[1], rev)))
            args.append(q[0])
        in_specs += [pl.BlockSpec((c, GLA_DK), rmap(k[1], rev)),
                     pl.BlockSpec((c, GLA_DV), rmap(v[1], rev)),
                     pl.BlockSpec((c, 128), rmap(gd[1], rev, 0)),
                     pl.BlockSpec((1, 128, GLA_DK), lambda b, h, s: (h, 0, 0)),
                     pl.BlockSpec((1, 1, GLA_DK), lambda b, h, s: (h, 0, 0))]
        args += [k[0], v[0], gd[0], wgs[d], bgs[d]]
        if has_init:
            in_specs.append(pl.BlockSpec((1, 1, GLA_DV, GLA_DK), lambda b, h, s: (b, h, 0, 0)))
            args.append(s0s[d])
    scratch = [pltpu.VMEM((GLA_DV, GLA_DK), F32)] * 2
    if with_out:
        out_specs = [pl.BlockSpec((c, GLA_DV), rmap(0, rev)) for rev in (False, True)]
        out_shape = [jax.ShapeDtypeStruct((batch * rows_per_batch, GLA_HEADS * GLA_DV), BF16)] * 2
    else:
        out_specs = [pl.BlockSpec((1, 1, GLA_DV, GLA_DK), lambda b, h, s: (b, h, 0, 0))] * 2
        out_shape = [jax.ShapeDtypeStruct((batch, GLA_HEADS, GLA_DV, GLA_DK), F32)] * 2
    body = functools.partial(_gla_body, with_out=with_out, has_init=has_init, nsteps=nsteps)
    return pl.pallas_call(
        body,
        grid=(batch, GLA_HEADS, nsteps),
        in_specs=in_specs,
        out_specs=out_specs,
        out_shape=out_shape,
        scratch_shapes=scratch,
        compiler_params=_cparams(("parallel", "parallel", "arbitrary")),
        name="gla_scan" if with_out else "gla_state",
    )(*args)


def _fft_a_body(x_ref, gc_ref, gs_ref, tr_ref, ti_ref, o_ref):
    n1, t2, ch = x_ref.shape
    x = x_ref[...].reshape(n1 * t2, ch)
    yr = jnp.dot(gc_ref[...], x, preferred_element_type=F32)
    yi = -jnp.dot(gs_ref[...], x, preferred_element_type=F32)
    tr, ti = tr_ref[...], ti_ref[...]
    o_ref[:, :, :ch] = (yr * tr - yi * ti).astype(BF16).reshape(n1, t2, ch)
    o_ref[:, :, ch:] = (yr * ti + yi * tr).astype(BF16).reshape(n1, t2, ch)


def _fft_b_body(y_ref, fc_ref, fs_ref, cc_ref, sc_ref, o_ref, *, norm):
    ch = y_ref.shape[1] // 2
    yr, yi = y_ref[:, :ch], y_ref[:, ch:]
    fc, fs = fc_ref[...], fs_ref[...]
    zr = jnp.dot(fc, yr, preferred_element_type=F32) + jnp.dot(fs, yi, preferred_element_type=F32)
    zi = jnp.dot(fc, yi, preferred_element_type=F32) - jnp.dot(fs, yr, preferred_element_type=F32)
    for g in range(FNET_GROUPS):
        sl = slice(g * FNET_DIM, (g + 1) * FNET_DIM)
        f = jnp.dot(zr[:, sl].astype(BF16), cc_ref[...], preferred_element_type=F32)
        f = f + jnp.dot(zi[:, sl].astype(BF16), sc_ref[...], preferred_element_type=F32)
        o_ref[:, sl] = (f * norm).astype(BF16)


def _dft_tables(s):
    n1, n2, t2 = FFT_N1, s // FFT_N1, FFT_T2
    a = np.arange(n1)
    ang1 = 2.0 * np.pi * ((a[:, None] * a[None, :]) % n1) / n1
    eye = np.eye(t2)
    gc = np.kron(np.cos(ang1), eye)
    gs = np.kron(np.sin(ang1), eye)
    b = np.arange(n2)
    angt = 2.0 * np.pi * ((a[:, None] * b[None, :]) % s) / s
    tw = angt.reshape(n1, n2 // t2, t2).transpose(1, 0, 2).reshape(n2 // t2, n1 * t2, 1)
    ang2 = 2.0 * np.pi * ((b[:, None] * b[None, :]) % n2) / n2
    d = np.arange(FNET_DIM)
    angc = 2.0 * np.pi * ((d[:, None] * d[None, :]) % FNET_DIM) / FNET_DIM
    bf = lambda m: jnp.asarray(m, dtype=BF16)
    return dict(gc=bf(gc), gs=bf(gs), tr=jnp.asarray(np.cos(tw), F32), ti=jnp.asarray(-np.sin(tw), F32),
                fc=bf(np.cos(ang2)), fs=bf(np.sin(ang2)), cc=bf(np.cos(angc)), sc=bf(np.sin(angc)))


def fourier_mix(p, u_col, batch, s):
    n1, n2, t2 = FFT_N1, s // FFT_N1, FFT_T2
    ch = FNET_GROUPS * FNET_DIM
    tabs = _dft_tables(s)
    p4 = p.reshape(batch, n1, n2, p.shape[1])
    full = lambda shape: pl.BlockSpec(shape, lambda b, j: (0,) * len(shape))
    y = pl.pallas_call(
        _fft_a_body,
        grid=(batch, n2 // t2),
        in_specs=[pl.BlockSpec((None, n1, t2, ch), lambda b, j: (b, 0, j, u_col // ch)),
                  full((n1 * t2, n1 * t2)), full((n1 * t2, n1 * t2)),
                  pl.BlockSpec((None, n1 * t2, 1), lambda b, j: (j, 0, 0)),
                  pl.BlockSpec((None, n1 * t2, 1), lambda b, j: (j, 0, 0))],
        out_specs=pl.BlockSpec((None, n1, t2, 2 * ch), lambda b, j: (b, 0, j, 0)),
        out_shape=jax.ShapeDtypeStruct((batch, n1, n2, 2 * ch), BF16),
        compiler_params=_cparams(("parallel", "parallel")),
        name="fft_stage_a",
    )(p4, tabs["gc"], tabs["gs"], tabs["tr"], tabs["ti"])
    norm = float(1.0 / np.sqrt(s * FNET_DIM))
    f = pl.pallas_call(
        functools.partial(_fft_b_body, norm=norm),
        grid=(batch, n1),
        in_specs=[pl.BlockSpec((None, None, n2, 2 * ch), lambda b, j: (b, j, 0, 0)),
                  full((n2, n2)), full((n2, n2)), full((FNET_DIM, FNET_DIM)), full((FNET_DIM, FNET_DIM))],
        out_specs=pl.BlockSpec((None, None, n2, ch), lambda b, j: (b, j, 0, 0)),
        out_shape=jax.ShapeDtypeStruct((batch, n1, n2, ch), BF16),
        compiler_params=_cparams(("parallel", "parallel")),
        name="fft_stage_b",
    )(y, tabs["fc"], tabs["fs"], tabs["cc"], tabs["sc"])
    return f.transpose(0, 2, 1, 3).reshape(batch * s, ch)


def _finish_body(of_ref, ob_ref, f_ref, g_ref, ong_ref, o_ref):
    dv = GLA_DV
    for h in range(GLA_HEADS):
        sl = slice(h * dv, (h + 1) * dv)
        o = of_ref[:, sl].astype(F32) + ob_ref[:, sl].astype(F32)
        y = _rms(o, ong_ref[...])
        gt = g_ref[:, sl].astype(F32)
        o_ref[:, sl] = (y * _silu(gt)).astype(BF16)
    base = GLA_HEADS * dv
    gt = g_ref[:, base:].astype(F32)
    o_ref[:, base:] = (f_ref[...].astype(F32) * _silu(gt)).astype(BF16)


def rec_finish(o_f, o_b, f, p, on_g):
    r = o_f.shape[0]
    tm = 256
    w = GLA_HEADS * GLA_DV
    mix = w + FNET_GROUPS * FNET_DIM
    return pl.pallas_call(
        _finish_body,
        grid=(r // tm,),
        in_specs=[pl.BlockSpec((tm, w), lambda i: (i, 0)),
                  pl.BlockSpec((tm, w), lambda i: (i, 0)),
                  pl.BlockSpec((tm, mix - w), lambda i: (i, 0)),
                  pl.BlockSpec((tm, mix), lambda i: (i, REC_GATE // mix)),
                  pl.BlockSpec((1, GLA_DV), lambda i: (0, 0))],
        out_specs=pl.BlockSpec((tm, mix), lambda i: (i, 0)),
        out_shape=jax.ShapeDtypeStruct((r, mix), BF16),
        compiler_params=_cparams(("parallel",)),
        name="rec_finish",
    )(o_f, o_b, f, p, on_g.reshape(1, -1))


def _rope_tables(s):
    t = np.arange(s)
    row, col = (t // GRID_W).astype(np.float32), (t % GRID_W).astype(np.float32)

    def cs(rot_dim):
        quarter = rot_dim // 4
        inv_freq = (np.float32(ROPE_THETA) ** (-np.arange(quarter, dtype=np.float32) / quarter)).astype(np.float32)
        ang = np.concatenate([row[:, None] * inv_freq, col[:, None] * inv_freq], axis=-1).astype(np.float32)
        return np.cos(ang), np.sin(ang)

    ca, sa = cs(GQA_HEAD_DIM)
    cb, sb = cs(MLA_ROPE)
    z32, z64 = np.zeros((s, 32), np.float32), np.zeros((s, 64), np.float32)
    tabs = [np.concatenate([ca, ca], 1), np.concatenate([-sa, sa], 1),
            np.concatenate([cb, cb, z64], 1), np.concatenate([-sb, z32, z64], 1), np.concatenate([z32, sb, z64], 1)]
    return [jnp.asarray(t_, F32) for t_ in tabs]


def _identity_rope_tables(n):
    one, zero = np.ones((n, 128), np.float32), np.zeros((n, 128), np.float32)
    cb = np.concatenate([np.ones((n, 64), np.float32), np.zeros((n, 64), np.float32)], 1)
    return [jnp.asarray(t_, F32) for t_ in (one, zero, cb, zero, zero)]


def attention_calls(p_lat, p_ctx, qg_l, qg_c, kg_l, kg_c, qm_l, qm_c, kv_l, kv_c, krp_l, krp_c, batch, s, n_ctx):
    va_col, gate_a_col, gate_b_col = ATT_VA // 128, ATT_GATE // 512, (ATT_GATE + 2048) // 128
    common = dict(batch=batch, s_lat=s, s_ctx=n_ctx)
    gqa = dict(heads=GQA_KV_HEADS, group=GQA_GROUP, dqk=GQA_HEAD_DIM, **common)
    mla = dict(heads=MLA_HEADS, group=1, dqk=MLA_QK_PAD, **common)
    kb_l, kb_c = [(kv_l, (0, 1)), (krp_l, (0, 0))], [(kv_c, (0, 1)), (krp_c, (0, 0))]
    ya_l = flash(qg_l, [(kg_l, (0, 1))], (p_lat, va_col), [(kg_c, (0, 1))], (p_ctx, va_col), (p_lat, gate_a_col),
                 tq=FLASH_TN, q_is_ctx=False, **gqa)
    yb_l = flash(qm_l, kb_l, (kv_l, MLA_HEADS), kb_c, (kv_c, MLA_HEADS), (p_lat, gate_b_col),
                 tq=4 * FLASH_TN, q_is_ctx=False, **mla)
    ya_c = flash(qg_c, None, None, [(kg_c, (0, 1))], (p_ctx, va_col), (p_ctx, gate_a_col),
                 tq=n_ctx, q_is_ctx=True, **gqa)
    yb_c = flash(qm_c, None, None, kb_c, (kv_c, MLA_HEADS), (p_ctx, gate_b_col), tq=n_ctx, q_is_ctx=True, **mla)
    return ya_l, ya_c, yb_l, yb_c


def _attention_layer(x_lat, x_ctx, mod, norm_g, w_in, qn_g, kn_g, cq_g, ckv_g, w_uq, w_ukv, w_out, batch, s, n_ctx):
    d = x_lat.shape[1]
    shift, scale, gate = mod[:, :d], mod[:, d:2 * d], mod[:, 2 * d:]
    h_lat = modnorm(x_lat, norm_g, scale[:batch], shift[:batch], s, BF16)
    h_ctx = modnorm(x_ctx, norm_g, scale[batch:batch + 1], shift[batch:batch + 1], batch * n_ctx, BF16)

    w_in_p = jnp.concatenate([w_in[:, :4672], jnp.zeros((d, ATT_GATE - 4672), w_in.dtype), w_in[:, 4672:]], axis=1).astype(BF16)
    w_uq_p = jnp.pad(w_uq.reshape(MLA_Q_LORA, MLA_HEADS, MLA_NOPE + MLA_ROPE),
                     ((0, 0), (0, 0), (0, MLA_QK_PAD - MLA_NOPE - MLA_ROPE))).reshape(MLA_Q_LORA, MLA_HEADS * MLA_QK_PAD).astype(BF16)
    w_ukv3 = w_ukv.reshape(MLA_KV_LORA, MLA_HEADS, MLA_NOPE + MLA_V)
    w_ukv_p = jnp.concatenate([w_ukv3[:, :, :MLA_NOPE].reshape(MLA_KV_LORA, -1),
                               w_ukv3[:, :, MLA_NOPE:].reshape(MLA_KV_LORA, -1)], axis=1).astype(BF16)
    w_out_b = w_out.astype(BF16)

    ctx_rows = batch * n_ctx
    p_lat = matmul(h_lat, w_in_p, 1024, 1024, BF16)
    p_ctx = matmul(h_ctx, w_in_p, ctx_rows, 1024, BF16)

    tabs_lat, tabs_ctx = _rope_tables(s), _identity_rope_tables(ctx_rows)
    qg_l, kg_l, cqn_l, ckvn_l, krp_l = attn_prep(p_lat, tabs_lat, s, qn_g, kn_g, cq_g, ckv_g)
    qg_c, kg_c, cqn_c, ckvn_c, krp_c = attn_prep(p_ctx, tabs_ctx, ctx_rows, qn_g, kn_g, cq_g, ckv_g)
    qm_l = mla_q(cqn_l, w_uq_p, tabs_lat[2:], s, 1024)
    qm_c = mla_q(cqn_c, w_uq_p, tabs_ctx[2:], ctx_rows, ctx_rows)
    kv_l = matmul(ckvn_l, w_ukv_p, 1024, 1024, BF16)
    kv_c = matmul(ckvn_c, w_ukv_p, ctx_rows, 1024, BF16)

    ya_l, ya_c, yb_l, yb_c = attention_calls(p_lat, p_ctx, qg_l, qg_c, kg_l, kg_c, qm_l, qm_c, kv_l, kv_c, krp_l, krp_c,
                                             batch, s, n_ctx)

    x_lat = matmul_residual(ya_l, yb_l, w_out_b, x_lat, gate[:batch], s, 1024, 512)
    x_ctx = matmul_residual(ya_c, yb_c, w_out_b, x_ctx, gate[batch:batch + 1], ctx_rows, ctx_rows, 512)
    return x_lat, x_ctx


def _recurrent_layer(x_lat, x_ctx, mod, norm_g, w_in, wg_f, bg_f, wg_b, bg_b, on_g, w_out, batch, s, n_ctx):
    d = x_lat.shape[1]
    shift, scale, gate = mod[:, :d], mod[:, d:2 * d], mod[:, 2 * d:]
    h_lat = modnorm(x_lat, norm_g, scale[:batch], shift[:batch], s, BF16)
    h_ctx = modnorm(x_ctx, norm_g, scale[batch:batch + 1], shift[batch:batch + 1], batch * n_ctx, BF16)

    w_main = jnp.concatenate([w_in[:, 7200:], w_in[:, :6144], w_in[:, 6176:7200]], axis=1).astype(BF16)
    w_gd = jnp.pad(w_in[:, 6144:6176], ((0, 0), (0, 96))).astype(BF16)
    w_out_b = w_out.astype(BF16)

    def gate_w(wg, row0):
        w3 = wg.reshape(GLA_GATE_RANK, GLA_HEADS, GLA_DK).transpose(1, 0, 2)
        return jnp.pad(w3, ((0, 0), (row0, 128 - GLA_GATE_RANK - row0), (0, 0))).astype(BF16)

    wgf_p, wgb_p = gate_w(wg_f, 0), gate_w(wg_b, GLA_GATE_RANK)
    bgf, bgb = bg_f.reshape(GLA_HEADS, 1, GLA_DK), bg_b.reshape(GLA_HEADS, 1, GLA_DK)

    ctx_rows = batch * n_ctx
    p_lat = matmul(h_lat, w_main, 1024, 1024, BF16)
    p_ctx = matmul(h_ctx, w_main, ctx_rows, 1024, BF16)
    gd_lat = matmul(h_lat, w_gd, 1024, 128, F32)
    gd_ctx = matmul(h_ctx, w_gd, ctx_rows, 128, F32)

    kq, kk, kv = REC_Q // GLA_DK, REC_K // GLA_DK, REC_V // GLA_DV
    wgs, bgs = (wgf_p, wgb_p), (bgf, bgb)
    states = gla(None, (p_ctx, kk), (p_ctx, kv), (gd_ctx, 0), wgs, bgs, None, batch=batch, rows_per_batch=n_ctx)
    o_f, o_b = gla((p_lat, kq), (p_lat, kk), (p_lat, kv), (gd_lat, 0), wgs, bgs, states, batch=batch, rows_per_batch=s)

    f = fourier_mix(p_lat, REC_U, batch, s)
    y = rec_finish(o_f, o_b, f, p_lat, on_g)
    return matmul_residual_single(y, w_out_b, x_lat, gate[:batch], s, 1024, 512)


def _mm_res1_body(a_ref, b_ref, x_ref, g_ref, o_ref):
    acc = jnp.dot(a_ref[...], b_ref[...], preferred_element_type=F32)
    o_ref[...] = x_ref[...] + g_ref[0] * acc


def matmul_residual_single(a, b, x, gate, rows_per_group, tm, tn):
    m, k = a.shape
    n = b.shape[1]
    ngroups = gate.shape[0]
    tiles_per_group = rows_per_group // tm
    return pl.pallas_call(
        _mm_res1_body,
        grid=(n // tn, m // tm),
        in_specs=[pl.BlockSpec((tm, k), lambda j, i: (i, 0)),
                  pl.BlockSpec((k, tn), lambda j, i: (0, j)),
                  pl.BlockSpec((tm, tn), lambda j, i: (i, j)),
                  pl.BlockSpec((1, 1, tn), lambda j, i: (jnp.minimum(i // tiles_per_group, ngroups - 1), 0, j))],
        out_specs=pl.BlockSpec((tm, tn), lambda j, i: (i, j)),
        out_shape=jax.ShapeDtypeStruct((m, n), F32),
        compiler_params=_cparams(("parallel", "parallel")),
        name="matmul_residual1",
    )(a, b, x, gate.reshape(ngroups, 1, n))


def kernel(x, c, ctx, c_ctx, norm_g, ada_w, ada_b, att_w_in, att_qn_g, att_kn_g, mla_cq_g, mla_ckv_g, mla_w_uq,
           mla_w_ukv, att_w_out, rec_w_in, gla_wg_f, gla_bg_f, gla_wg_b, gla_bg_b, gla_on_g, rec_w_out, final_g):
    batch, s, d = x.shape
    n_ctx = ctx.shape[1]
    x_lat = x.reshape(batch * s, d)
    x_ctx = ctx.reshape(batch * n_ctx, d)
    cond = jnp.concatenate([c, c_ctx[None, :], jnp.zeros((8 - batch - 1, d), F32)], axis=0)
    mods = adaln(cond, ada_w, ada_b)

    x_lat, x_ctx = _attention_layer(x_lat, x_ctx, mods[0], norm_g[0], att_w_in[0], att_qn_g[0], att_kn_g[0],
                                    mla_cq_g[0], mla_ckv_g[0], mla_w_uq[0], mla_w_ukv[0], att_w_out[0],
                                    batch, s, n_ctx)
    x_lat = _recurrent_layer(x_lat, x_ctx, mods[1], norm_g[1], rec_w_in[0], gla_wg_f[0], gla_bg_f[0], gla_wg_b[0],
                             gla_bg_b[0], gla_on_g[0], rec_w_out[0], batch, s, n_ctx)
    zero = jnp.zeros((1, d), F32)
    out = modnorm(x_lat, final_g, zero, zero, batch * s, F32)
    return out.reshape(batch, s, d)
```
